```python
import jax, jax.numpy as jnp
from jax import lax
import numpy as np

D_MODEL = 2048
BATCH = 16
SEQ = 256
DEPTH = 1
DEC_BATCH = 8
DEC_SEQ = 4096
PAST_LEN = 256

GRID_W = 64
D_FOURIER = D_MODEL // 2
F_GROUPS = 4
F_GROUP_W = D_FOURIER // F_GROUPS
D_RWKV = D_MODEL // 2
HEAD_SIZE = 64
N_RWKV_HEADS = D_RWKV // HEAD_SIZE
DECAY_LORA = 96
ICLR_LORA = 96
GATE_LORA = 256
DECAY_SCALE = 0.606531
GN_EPS = 64e-5
N_DIR = 2
D_IN = D_FOURIER + 3 * D_RWKV + 2 * D_MODEL
N_EXPERTS = 256
TOP_K = 8
N_EXPERT_GROUPS = 8
TOPK_GROUPS = 4
D_EXPERT = D_MODEL // 4
ROUTED_SCALE = 2.5
MOE_BLOCK = 128
RMS_EPS = 1e-6

kernel_name = 'hybrid_fourier_rwkv7_moe_dit_step'


def rms_norm(x, g):
    x32 = x.astype(jnp.float32)
    y = x32 * lax.rsqrt(jnp.mean(x32 * x32, axis=-1, keepdims=True) + RMS_EPS)
    return (y * g.astype(jnp.float32)).astype(x.dtype)


def centred_conv3(u, w):
    up = jnp.pad(u, ((0, 0), (1, 1), (0, 0)))
    return up[:, :-2] * w[0] + up[:, 1:-1] * w[1] + up[:, 2:] * w[2]


def fourier_mix(u, on_grid):
    bn, L, _ = u.shape
    u32 = u.astype(jnp.float32)
    if on_grid:
        rows = L // GRID_W
        z = u32.reshape(bn, rows, GRID_W, F_GROUPS, F_GROUP_W)
        axes = (1, 2, 4)
    else:
        z = u32.reshape(bn, L, F_GROUPS, F_GROUP_W)
        axes = (1, 3)
    y = jnp.fft.fftn(z, axes=axes, norm='ortho').real
    return y.reshape(bn, L, D_FOURIER).astype(u.dtype)


def _rwkv7_step(S, inp):
    r_t, w_t, k_t, v_t, kk_t, a_t = inp
    s_kk = jnp.einsum('zbhvk,zbhk->zbhv', S, kk_t)
    S = (S * w_t[..., None, :]
         - s_kk[..., :, None] * (kk_t * a_t)[..., None, :]
         + v_t[..., :, None] * k_t[..., None, :])
    return S, jnp.einsum('zbhvk,zbhk->zbhv', S, r_t)


def rwkv7_bidir(h, u_rkv, state0, p):
    bn, L, _ = h.shape
    f32 = jnp.float32
    rkv = centred_conv3(u_rkv, p['conv_rkv']).astype(f32)
    r, k, v = jnp.split(rkv, 3, axis=-1)
    hf = h.astype(f32)
    w_logit = p['decay_bias'][:, None, None, :] + jnp.einsum(
        'zblr,zrc->zblc', jnp.tanh(jnp.einsum('bld,zdr->zblr', hf, p['decay_a'])), p['decay_b'])
    decay = jnp.exp(-DECAY_SCALE * jax.nn.sigmoid(w_logit.astype(f32)))
    a = jax.nn.sigmoid((p['iclr_bias'][:, None, None, :] + jnp.einsum(
        'zblr,zrc->zblc', jnp.einsum('bld,zdr->zblr', hf, p['iclr_a']), p['iclr_b'])).astype(f32))
    g = jax.nn.sigmoid(hf @ p['gate_a']) @ p['gate_b']

    def heads(t):
        return t.reshape(t.shape[:-1] + (N_RWKV_HEADS, HEAD_SIZE))

    kk = heads(k * p['k_k'])
    kk = kk / jnp.maximum(jnp.sqrt(jnp.sum(kk * kk, axis=-1, keepdims=True)), 1e-12)
    k_dir = heads(k[None] * (1.0 + (a - 1.0) * p['k_a']))
    r_h, v_h = heads(r), heads(v)

    def both(t):
        return jnp.stack([t, jnp.flip(t, 1)])

    def orient(t):
        return jnp.stack([t[0], jnp.flip(t[1], 1)])

    xs = (both(r_h), orient(heads(decay)), orient(k_dir), both(v_h), both(kk), orient(heads(a)))
    xs = tuple(jnp.moveaxis(t, 2, 0).astype(f32) for t in xs)
    s_final, o = lax.scan(_rwkv7_step, state0.astype(f32), xs)
    o = jnp.moveaxis(o, 0, 2)
    o = o[0] + jnp.flip(o[1], 1)
    mu = jnp.mean(o, axis=-1, keepdims=True)
    var = jnp.mean(jnp.square(o - mu), axis=-1, keepdims=True)
    o = ((o - mu) * lax.rsqrt(var + GN_EPS)).reshape(bn, L, D_RWKV) * p['ln_x_w'] + p['ln_x_b']
    bonus = jnp.sum(r_h[None] * k_dir * p['r_k'], axis=(0, -1))[..., None] * v_h
    y = (o + bonus.reshape(bn, L, D_RWKV)) * g
    return y.astype(h.dtype), s_final


def token_mixer(h, on_grid, state0, p):
    proj = jnp.einsum('bld,de->ble', h, p['w_in'])
    u_f = proj[..., :D_FOURIER]
    u_rkv = proj[..., D_FOURIER:D_FOURIER + 3 * D_RWKV]
    g_f, g_r = jnp.split(proj[..., D_FOURIER + 3 * D_RWKV:], 2, axis=-1)
    y_f = fourier_mix(u_f, on_grid) @ p['w_fourier_out']
    y_r, state = rwkv7_bidir(h, u_rkv, state0, p)
    y_r = y_r @ p['w_rwkv_out']
    merged = jax.nn.sigmoid(g_f) * y_f + jax.nn.sigmoid(g_r) * y_r
    return merged @ p['w_out'], state


def moe_ffn(h, p):
    T = h.shape[0]
    f32 = jnp.float32
    n_per_group = N_EXPERTS // N_EXPERT_GROUPS
    scores = jax.nn.sigmoid(h.astype(f32) @ p['w_router'].astype(f32))
    sel = scores + p['router_bias'].astype(f32)
    grp = sel.reshape(T, N_EXPERT_GROUPS, n_per_group)
    grp_score = jnp.sum(lax.top_k(grp, 2)[0], axis=-1)
    _, gidx = lax.top_k(grp_score, TOPK_GROUPS)
    gmask = jnp.sum(jax.nn.one_hot(gidx, N_EXPERT_GROUPS, dtype=f32), axis=1) > 0
    sel = jnp.where(jnp.repeat(gmask, n_per_group, axis=1), sel, -jnp.inf)
    _, eidx = lax.top_k(sel, TOP_K)
    wts = jnp.take_along_axis(scores, eidx, axis=-1)
    wts = wts / jnp.sum(wts, axis=-1, keepdims=True) * ROUTED_SCALE

    A = T * TOP_K
    eid = eidx.reshape(-1).astype(jnp.int32)
    tok = jnp.repeat(jnp.arange(T, dtype=jnp.int32), TOP_K)
    wt = wts.reshape(-1)
    order = jnp.argsort(eid)
    eid_s, tok_s, wt_s = eid[order], tok[order], wt[order]
    counts = jnp.bincount(eid, length=N_EXPERTS).astype(jnp.int32)
    starts = jnp.cumsum(counts) - counts
    nblk = (counts + MOE_BLOCK - 1) // MOE_BLOCK
    blk_end = jnp.cumsum(nblk)
    blk_start = blk_end - nblk
    dest = blk_start[eid_s] * MOE_BLOCK + (jnp.arange(A, dtype=jnp.int32) - starts[eid_s])
    nb = -(-A // MOE_BLOCK) + N_EXPERTS
    buf_tok = jnp.full((nb * MOE_BLOCK,), T, jnp.int32).at[dest].set(tok_s)
    buf_wt = jnp.zeros((nb * MOE_BLOCK,), h.dtype).at[dest].set(wt_s.astype(h.dtype))
    blk_exp = jnp.minimum(jnp.searchsorted(blk_end, jnp.arange(nb, dtype=jnp.int32), side='right'),
                          N_EXPERTS - 1)
    x_pad = jnp.concatenate([h, jnp.zeros((1, D_MODEL), h.dtype)], axis=0)
    w1, w3, w2 = p['w1_exp'], p['w3_exp'], p['w2_exp']

    def block(acc, inp):
        toks, bw, e = inp
        xb = x_pad[toks]
        hb = jax.nn.silu(xb @ w1[e]) * (xb @ w3[e])
        return acc.at[toks].add((hb @ w2[e]) * bw[:, None]), None

    acc, _ = lax.scan(block, jnp.zeros((T + 1, D_MODEL), h.dtype),
                      (buf_tok.reshape(nb, MOE_BLOCK), buf_wt.reshape(nb, MOE_BLOCK), blk_exp))
    shared = (jax.nn.silu(h @ p['w1_sh']) * (h @ p['w3_sh'])) @ p['w2_sh']
    return acc[:T] + shared


def trunk_layer(x, cond, on_grid, state0, p):
    mod = jnp.einsum('bd,de->be', jax.nn.silu(cond), p['w_ada']) + p['b_ada']
    shift1, scale1, gate1, shift2, scale2, gate2 = jnp.split(mod[:, None, :], 6, axis=-1)
    h = rms_norm(x, p['g_pre_mix']) * (1.0 + scale1) + shift1
    m, state = token_mixer(h, on_grid, state0, p)
    x = x + gate1 * rms_norm(m, p['g_post_mix'])
    h = rms_norm(x, p['g_pre_ffn']) * (1.0 + scale2) + shift2
    f = moe_ffn(h.reshape(-1, D_MODEL), p).reshape(x.shape)
    x = x + gate2 * rms_norm(f, p['g_post_ffn'])
    return x, state


def setup_inputs(seed: int = 0) -> dict:
    key = jax.random.key(seed)
    ks = iter(jax.random.split(key, 48))

    def nrm(shape, scale):
        return jax.random.normal(next(ks), shape, jnp.float32) * scale

    def gain(shape):
        return 1.0 + nrm(shape, 0.05)

    D = D_MODEL
    L = DEPTH
    H, N = N_RWKV_HEADS, HEAD_SIZE
    return {
        'x_prompt': nrm((BATCH, SEQ, D), 1.0),
        'x_sample': nrm((DEC_BATCH, DEC_SEQ, D), 1.0),
        'state_rwkv': nrm((DEC_BATCH, L, N_DIR, H, N, N), 0.3),
        'c': nrm((DEC_BATCH, D), 1.0),
        'c_ctx': nrm((D,), 1.0),
        'w_ada': nrm((L, D, 6 * D), 0.5 * D ** -0.5),
        'b_ada': nrm((L, 6 * D), 0.02),
        'g_pre_mix': gain((L, D)),
        'g_post_mix': gain((L, D)),
        'g_pre_ffn': gain((L, D)),
        'g_post_ffn': gain((L, D)),
        'w_in': nrm((L, D, D_IN), D ** -0.5),
        'conv_rkv': jnp.array([0.25, 0.5, 0.25], jnp.float32)[None, :, None] + nrm((L, 3, 3 * D_RWKV), 0.1),
        'decay_a': nrm((L, N_DIR, D, DECAY_LORA), D ** -0.5),
        'decay_b': nrm((L, N_DIR, DECAY_LORA, D_RWKV), DECAY_LORA ** -0.5),
        'decay_bias': nrm((L, N_DIR, D_RWKV), 0.5),
        'iclr_a': nrm((L, N_DIR, D, ICLR_LORA), D ** -0.5),
        'iclr_b': nrm((L, N_DIR, ICLR_LORA, D_RWKV), ICLR_LORA ** -0.5),
        'iclr_bias': nrm((L, N_DIR, D_RWKV), 0.1),
        'gate_a': nrm((L, D, GATE_LORA), D ** -0.5),
        'gate_b': nrm((L, GATE_LORA, D_RWKV), GATE_LORA ** -0.5),
        'k_k': 0.85 + nrm((L, D_RWKV), 0.05),
        'k_a': gain((L, D_RWKV)),
        'r_k': nrm((L, H, N), 0.1),
        'ln_x_w': gain((L, D_RWKV)),
        'ln_x_b': nrm((L, D_RWKV), 0.02),
        'w_fourier_out': nrm((L, D_FOURIER, D), D_FOURIER ** -0.5),
        'w_rwkv_out': nrm((L, D_RWKV, D), D_RWKV ** -0.5),
        'w_out': nrm((L, D, D), D ** -0.5),
        'w_router': nrm((L, D, N_EXPERTS), D ** -0.5),
        'router_bias': nrm((L, N_EXPERTS), 0.01),
        'w1_exp': nrm((L, N_EXPERTS, D, D_EXPERT), D ** -0.5),
        'w3_exp': nrm((L, N_EXPERTS, D, D_EXPERT), D ** -0.5),
        'w2_exp': nrm((L, N_EXPERTS, D_EXPERT, D), D_EXPERT ** -0.5),
        'w1_sh': nrm((L, D, D_EXPERT), D ** -0.5),
        'w3_sh': nrm((L, D, D_EXPERT), D ** -0.5),
        'w2_sh': nrm((L, D_EXPERT, D), D_EXPERT ** -0.5),
    }


def reference(x_prompt, x_sample, state_rwkv, c, c_ctx, w_ada, b_ada, g_pre_mix, g_post_mix,
              g_pre_ffn, g_post_ffn, w_in, conv_rkv, decay_a, decay_b, decay_bias, iclr_a, iclr_b,
              iclr_bias, gate_a, gate_b, k_k, k_a, r_k, ln_x_w, ln_x_b, w_fourier_out, w_rwkv_out,
              w_out, w_router, router_bias, w1_exp, w3_exp, w2_exp, w1_sh, w3_sh, w2_sh):
    xp = x_prompt
    xs = x_sample
    new_states = []
    for l in range(DEPTH):
        p = {
            'w_ada': w_ada[l], 'b_ada': b_ada[l],
            'g_pre_mix': g_pre_mix[l], 'g_post_mix': g_post_mix[l],
            'g_pre_ffn': g_pre_ffn[l], 'g_post_ffn': g_post_ffn[l],
            'w_in': w_in[l], 'conv_rkv': conv_rkv[l],
            'decay_a': decay_a[l], 'decay_b': decay_b[l], 'decay_bias': decay_bias[l],
            'iclr_a': iclr_a[l], 'iclr_b': iclr_b[l], 'iclr_bias': iclr_bias[l],
            'gate_a': gate_a[l], 'gate_b': gate_b[l],
            'k_k': k_k[l], 'k_a': k_a[l], 'r_k': r_k[l],
            'ln_x_w': ln_x_w[l], 'ln_x_b': ln_x_b[l],
            'w_fourier_out': w_fourier_out[l], 'w_rwkv_out': w_rwkv_out[l], 'w_out': w_out[l],
            'w_router': w_router[l], 'router_bias': router_bias[l],
            'w1_exp': w1_exp[l], 'w3_exp': w3_exp[l], 'w2_exp': w2_exp[l],
            'w1_sh': w1_sh[l], 'w3_sh': w3_sh[l], 'w2_sh': w2_sh[l],
        }
        ctx_state0 = jnp.zeros((N_DIR, xp.shape[0], N_RWKV_HEADS, HEAD_SIZE, HEAD_SIZE), jnp.float32)
        xp, ctx_state = trunk_layer(xp, c_ctx[None, :], False, ctx_state0, p)
        new_states.append(jnp.moveaxis(ctx_state, 0, 1))
        lat_state0 = jnp.moveaxis(state_rwkv[:, l], 1, 0)
        xs, _ = trunk_layer(xs, c, True, lat_state0, p)
    state_rwkv_new = jnp.stack(new_states, axis=1)
    return (xp, xs, state_rwkv_new)
```

```python
import functools

import numpy as np
import jax
import jax.numpy as jnp
from jax import lax
from jax.experimental import pallas as pl
from jax.experimental.pallas import tpu as pltpu

F32 = jnp.float32
BF16 = jnp.bfloat16

HEAD = 64
LANES = 128
TM = 256
CHUNK = 64
DECAY_SCALE = 0.606531
GN_EPS = 64e-5
RMS_EPS = 1e-6
N_GROUPS = 8
TOPK_GROUPS = 4
TOP_K = 8
ROUTED_SCALE = 2.5
GRID_W = 64
MOE_TM = 256
CMB_TM = 128
VMEM_LIMIT = 56 * 1024 * 1024


def _cparams(sem):
    return pltpu.CompilerParams(dimension_semantics=sem, vmem_limit_bytes=VMEM_LIMIT)


def _dot(a, b):
    return jnp.dot(a, b, preferred_element_type=F32)


def _dot_nt(a, b):
    return lax.dot_general(a, b, (((1,), (1,)), ((), ())), preferred_element_type=F32)


def _split2(x):
    hi = x.astype(BF16)
    lo = (x - hi.astype(F32)).astype(BF16)
    return hi, lo


def _rms(x, g):
    return x * lax.rsqrt(jnp.mean(x * x, axis=-1, keepdims=True) + RMS_EPS) * g


def _seq_row(i, tile, n_ctx_rows, lat_len):
    t_ctx = n_ctx_rows // tile
    per = lat_len // tile
    return jnp.where(i < t_ctx, 0, 1 + (i - t_ctx) // per)


def _ada_kernel(c_ref, w_ref, b_ref, o_ref):
    c = c_ref[...]
    s = c * jax.nn.sigmoid(c)
    o_ref[...] = _dot(s.astype(BF16), w_ref[...]) + b_ref[...]


def _ada(cond, w, b):
    m, d = cond.shape
    n = w.shape[1]
    tn = 2048
    return pl.pallas_call(
        _ada_kernel,
        grid=(n // tn,),
        in_specs=[pl.BlockSpec((m, d), lambda j: (0, 0)),
                  pl.BlockSpec((d, tn), lambda j: (0, j)),
                  pl.BlockSpec((1, tn), lambda j: (0, j))],
        out_specs=pl.BlockSpec((m, tn), lambda j: (0, j)),
        out_shape=jax.ShapeDtypeStruct((m, n), F32),
        compiler_params=_cparams(("arbitrary",)),
        name="ada",
    )(cond, w, b)


def _prenorm_kernel(x_ref, g_ref, mod_ref, h_ref):
    m = mod_ref[0]
    y = _rms(x_ref[...], g_ref[...])
    h_ref[...] = (y * (1.0 + m[1:2]) + m[0:1]).astype(h_ref.dtype)


def _prenorm(x, g, mod, n_ctx_rows, lat_len):
    t, d = x.shape
    row = functools.partial(_seq_row, tile=TM, n_ctx_rows=n_ctx_rows, lat_len=lat_len)
    return pl.pallas_call(
        _prenorm_kernel,
        grid=(t // TM,),
        in_specs=[pl.BlockSpec((TM, d), lambda i: (i, 0)),
                  pl.BlockSpec((1, d), lambda i: (0, 0)),
                  pl.BlockSpec((1, 6, d), lambda i: (row(i), 0, 0))],
        out_specs=pl.BlockSpec((TM, d), lambda i: (i, 0)),
        out_shape=jax.ShapeDtypeStruct((t, d), BF16),
        compiler_params=_cparams(("arbitrary",)),
        name="prenorm",
    )(x, g, mod)


def _mm_kernel(x_ref, w_ref, o_ref):
    o_ref[...] = _dot(x_ref[...], w_ref[...]).astype(o_ref.dtype)


def _matmul(x, w, tm, tn, out_dtype, name):
    m, k = x.shape
    n = w.shape[1]
    return pl.pallas_call(
        _mm_kernel,
        grid=(n // tn, m // tm),
        in_specs=[pl.BlockSpec((tm, k), lambda j, i: (i, 0)),
                  pl.BlockSpec((k, tn), lambda j, i: (0, j))],
        out_specs=pl.BlockSpec((tm, tn), lambda j, i: (i, j)),
        out_shape=jax.ShapeDtypeStruct((m, n), out_dtype),
        compiler_params=_cparams(("arbitrary", "arbitrary")),
        name=name,
    )(x, w)


def _dft_mats(n):
    k = np.arange(n)
    ang = 2.0 * np.pi * ((k[:, None] * k[None, :]) % n) / n
    s = 1.0 / np.sqrt(n)
    return np.cos(ang) * s, np.sin(ang) * s


def _fourier_ctx_kernel(z_ref, cs_ref, ls_ref, y_ref):
    gw = z_ref.shape[1]
    ab = _dot(z_ref[...].astype(BF16), cs_ref[...])
    st = jnp.concatenate([ab[:, :gw], ab[:, gw:]], axis=0).astype(BF16)
    y_ref[...] = _dot(ls_ref[...], st).astype(y_ref.dtype)


def _fourier_ctx(proj, col0, n_seq, seq_len, gw, n_groups):
    cc, sc = _dft_mats(gw)
    cl, sl = _dft_mats(seq_len)
    cs = jnp.asarray(np.concatenate([cc, -sc], axis=1), BF16)
    ls = jnp.asarray(np.concatenate([cl, sl], axis=1), BF16)
    cb = col0 // gw
    return pl.pallas_call(
        _fourier_ctx_kernel,
        grid=(n_seq, n_groups),
        in_specs=[pl.BlockSpec((seq_len, gw), lambda b, g: (b, cb + g)),
                  pl.BlockSpec((gw, 2 * gw), lambda b, g: (0, 0)),
                  pl.BlockSpec((seq_len, 2 * seq_len), lambda b, g: (0, 0))],
        out_specs=pl.BlockSpec((seq_len, gw), lambda b, g: (b, g)),
        out_shape=jax.ShapeDtypeStruct((n_seq * seq_len, n_groups * gw), BF16),
        compiler_params=_cparams(("arbitrary", "arbitrary")),
        name="fourier_ctx",
    )(proj, cs, ls)


def _fourier_lat_a_kernel(z_ref, cs_ref, qs_ref, o_ref):
    gw = z_ref.shape[1]
    ab = _dot(z_ref[...].astype(BF16), cs_ref[...])
    for rr in range(TM // GRID_W):
        slab = ab[rr * GRID_W:(rr + 1) * GRID_W]
        swap = jnp.concatenate([slab[:, gw:], -slab[:, :gw]], axis=1)
        st = jnp.concatenate([slab, swap], axis=0).astype(BF16)
        out = _dot(qs_ref[...], st)
        o_ref[0, rr * GRID_W:(rr + 1) * GRID_W, :] = out[:, :gw].astype(o_ref.dtype)
        o_ref[1, rr * GRID_W:(rr + 1) * GRID_W, :] = out[:, gw:].astype(o_ref.dtype)


def _fourier_lat_b_kernel(x_ref, rs_ref, y_ref):
    st = jnp.concatenate([x_ref[0, 0], x_ref[1, 0]], axis=0)
    y_ref[0] = _dot(rs_ref[...], st).astype(y_ref.dtype)


def _fourier_lat(proj, col0, row0, n_seq, seq_len, gw, n_groups):
    rows = seq_len // GRID_W
    cc, sc = _dft_mats(gw)
    cq, sq = _dft_mats(GRID_W)
    cr, sr = _dft_mats(rows)
    cs = jnp.asarray(np.concatenate([cc, -sc], axis=1), BF16)
    qs = jnp.asarray(np.concatenate([cq, sq], axis=1), BF16)
    rs = jnp.asarray(np.concatenate([cr, sr], axis=1), BF16)
    cb = col0 // gw
    rb = row0 // TM
    per = seq_len // TM
    width = n_groups * gw
    ab = pl.pallas_call(
        _fourier_lat_a_kernel,
        grid=(n_seq * per, n_groups),
        in_specs=[pl.BlockSpec((TM, gw), lambda i, g: (rb + i, cb + g)),
                  pl.BlockSpec((gw, 2 * gw), lambda i, g: (0, 0)),
                  pl.BlockSpec((GRID_W, 2 * GRID_W), lambda i, g: (0, 0))],
        out_specs=pl.BlockSpec((2, TM, gw), lambda i, g: (0, i, g)),
        out_shape=jax.ShapeDtypeStruct((2, n_seq * seq_len, width), BF16),
        compiler_params=_cparams(("arbitrary", "arbitrary")),
        name="fourier_lat_cols",
    )(proj, cs, qs)
    flat = GRID_W * width
    tn = min(4096, flat)
    y = pl.pallas_call(
        _fourier_lat_b_kernel,
        grid=(n_seq, flat // tn),
        in_specs=[pl.BlockSpec((2, 1, rows, tn), lambda b, j: (0, b, 0, j)),
                  pl.BlockSpec((rows, 2 * rows), lambda b, j: (0, 0))],
        out_specs=pl.BlockSpec((1, rows, tn), lambda b, j: (b, 0, j)),
        out_shape=jax.ShapeDtypeStruct((n_seq, rows, flat), BF16),
        compiler_params=_cparams(("arbitrary", "arbitrary")),
        name="fourier_lat_rows",
    )(ab.reshape(2, n_seq, rows, flat), rs)
    return y.reshape(n_seq * seq_len, width)


def _group_sum(x, ones_ref):
    hi, lo = _split2(x)
    return _dot(hi, ones_ref[...]) + _dot(lo, ones_ref[...])


def _rwkv_prep_kernel(u_ref, up_ref, un_ref, lora_ref, conv_ref, db_ref, ib_ref, gb_ref,
                      dbias_ref, ibias_ref, kkw_ref, ka_ref, rk_ref, ones_ref,
                      r_o, v_o, kk_o, lw0_o, lw1_o, k0_o, k1_o, b0_o, b1_o, g_o, bv_o,
                      *, t_ctx, per):
    i = pl.program_id(0)
    dr = r_o.shape[0] * LANES
    j = i - t_ctx
    is_first = jnp.logical_or(i < t_ctx, lax.rem(j, per) == 0)
    is_last = jnp.logical_or(i < t_ctx, lax.rem(j, per) == per - 1)
    u = u_ref[...]
    rows = lax.broadcasted_iota(jnp.int32, (TM, 1), 0)
    prev_row = jnp.where(is_first, 0.0, up_ref[7:8, :])
    next_row = jnp.where(is_last, 0.0, un_ref[0:1, :])
    u_prev = jnp.where(rows == 0, prev_row, pltpu.roll(u, 1, 0))
    u_next = jnp.where(rows == TM - 1, next_row, pltpu.roll(u, TM - 1, 0))
    cw = conv_ref[...]
    rkv = u_prev * cw[0:1] + u * cw[1:2] + u_next * cw[2:3]
    r = rkv[:, :dr]
    k = rkv[:, dr:2 * dr]
    v = rkv[:, 2 * dr:]

    lora = lora_ref[...]
    dbias = dbias_ref[...]
    ibias = ibias_ref[...]
    lws, avs = [], []
    for z in range(2):
        dz = jnp.tanh(lora[:, z * LANES:(z + 1) * LANES]).astype(BF16)
        lws.append(-DECAY_SCALE * jax.nn.sigmoid(dbias[z:z + 1] + _dot(dz, db_ref[z])))
        iz = lora[:, (2 + z) * LANES:(3 + z) * LANES].astype(BF16)
        avs.append(jax.nn.sigmoid(ibias[z:z + 1] + _dot(iz, ib_ref[z])))
    gate = _dot(jax.nn.sigmoid(lora[:, 4 * LANES:6 * LANES]).astype(BF16), gb_ref[...])

    kkr = k * kkw_ref[...]
    ka = ka_ref[...]
    k0 = k * (1.0 + (avs[0] - 1.0) * ka)
    k1 = k * (1.0 + (avs[1] - 1.0) * ka)
    bon_in = r * (k0 + k1) * rk_ref[...]
    g_o[...] = gate
    for p in range(dr // LANES):
        sl = slice(p * LANES, (p + 1) * LANES)
        kk_p = kkr[:, sl]
        ss = _group_sum(kk_p * kk_p, ones_ref)
        kk_p = kk_p / jnp.maximum(jnp.sqrt(ss), 1e-12)
        r_o[p] = r[:, sl]
        v_o[p] = v[:, sl]
        kk_o[p] = kk_p
        lw0_o[p] = lws[0][:, sl]
        lw1_o[p] = lws[1][:, sl]
        k0_o[p] = k0[:, sl]
        k1_o[p] = k1[:, sl]
        b0_o[p] = kk_p * avs[0][:, sl]
        b1_o[p] = kk_p * avs[1][:, sl]
        bv_o[:, sl] = _group_sum(bon_in[:, sl], ones_ref) * v[:, sl]


def _head_block_ones(scale):
    m = np.zeros((LANES, LANES), np.float32)
    m[:HEAD, :HEAD] = scale
    m[HEAD:, HEAD:] = scale
    return jnp.asarray(m, BF16)


def _rwkv_prep(proj, lora_col, t, dr, n_ctx_rows, lat_len, conv, db, ib, gb, dbias, ibias, kkw, ka, rk):
    t_ctx = n_ctx_rows // TM
    per = lat_len // TM
    n_tiles = t // TM
    npair = dr // LANES
    w3 = 3 * dr
    lb = lora_col // 1024
    nb8 = t // 8
    pair = jax.ShapeDtypeStruct((npair, t, LANES), F32)
    flat = jax.ShapeDtypeStruct((t, dr), F32)
    full = lambda a: pl.BlockSpec(a.shape, lambda i, _n=a.ndim: (0,) * _n)
    ones = _head_block_ones(1.0)
    pair_spec = pl.BlockSpec((npair, TM, LANES), lambda i: (0, i, 0))
    flat_spec = pl.BlockSpec((TM, dr), lambda i: (i, 0))
    return pl.pallas_call(
        functools.partial(_rwkv_prep_kernel, t_ctx=t_ctx, per=per),
        grid=(n_tiles,),
        in_specs=[pl.BlockSpec((TM, w3), lambda i: (i, 0)),
                  pl.BlockSpec((8, w3), lambda i: (jnp.maximum(i * (TM // 8) - 1, 0), 0)),
                  pl.BlockSpec((8, w3), lambda i: (jnp.minimum((i + 1) * (TM // 8), nb8 - 1), 0)),
                  pl.BlockSpec((TM, 1024), lambda i: (i, lb)),
                  full(conv), full(db), full(ib), full(gb), full(dbias), full(ibias),
                  full(kkw), full(ka), full(rk), full(ones)],
        out_specs=[pair_spec] * 9 + [flat_spec, flat_spec],
        out_shape=[pair] * 9 + [flat, flat],
        compiler_params=_cparams(("arbitrary",)),
        name="rwkv_prep",
    )(proj, proj, proj, proj, conv, db, ib, gb, dbias, ibias, kkw, ka, rk, ones)


def _scan_chunk(r, v, kk, lw, k, b, s_mat, rev, tri, strict, incl, m0):
    c = r.shape[0]
    h0 = lw.astype(BF16)
    r1 = lw - h0.astype(F32)
    h1 = r1.astype(BF16)
    h2 = (r1 - h1.astype(F32)).astype(BF16)
    cum = _dot(tri, h0) + _dot(tri, h1) + _dot(tri, h2)
    p_in = jnp.exp(cum)
    p_ex = jnp.exp(cum - lw)
    p_inv = jnp.exp(-cum)
    p_tot = p_in[0:1] if rev else p_in[c - 1:c]

    def two_heads(x):
        return jnp.concatenate([jnp.where(m0, x, 0.0), jnp.where(m0, 0.0, x)], axis=0)

    at = two_heads(-kk * p_ex)
    rt = two_heads(r * p_in)
    bt = two_heads(b * p_inv)
    kt = two_heads(k * p_inv)
    vb = two_heads(v)
    lhs = jnp.concatenate([at, rt], axis=0).astype(BF16)
    rhs = jnp.concatenate([bt, kt], axis=0).astype(BF16)
    gram = _dot_nt(lhs, rhs)
    n2 = 2 * c
    l_ba = jnp.where(strict, gram[:n2, :n2], 0.0)
    l_ka = jnp.where(strict, gram[:n2, n2:], 0.0)
    m_br = jnp.where(incl, gram[n2:, :n2], 0.0)
    m_kr = jnp.where(incl, gram[n2:, n2:], 0.0)
    a_s = _dot_nt(lhs, s_mat.astype(BF16))
    vb16 = vb.astype(BF16)
    x = a_s[:n2] + _dot(l_ka.astype(BF16), vb16)
    lp = l_ba.astype(BF16)
    n_iter = int(np.log2(c))
    for it in range(n_iter):
        if it < n_iter - 1:
            both = _dot(lp, jnp.concatenate([lp, x.astype(BF16)], axis=1))
            lp = both[:, :n2].astype(BF16)
            x = x + both[:, n2:]
        else:
            x = x + _dot(lp, x.astype(BF16))
    u16 = x.astype(BF16)
    o_bd = a_s[n2:] + _dot(jnp.concatenate([m_br, m_kr], axis=1).astype(BF16),
                           jnp.concatenate([u16, vb16], axis=0))
    o = o_bd[:c] + o_bd[c:]
    uvt = jnp.concatenate([x.T, vb.T], axis=1).astype(BF16)
    s_new = (s_mat + _dot(uvt, rhs)) * p_tot
    return o, s_new


def _rwkv_scan_kernel(ft_ref, bt_ref, sq_ref, first_ref, last_ref,
                      rf, vf, kkf, lwf, kf, bf, rb, vb, kkb, lwb, kb, bb, s0_ref,
                      of_ref, ob_ref, sout_ref, s_scr):
    it = pl.program_id(1)

    @pl.when(first_ref[it] == 1)
    def _():
        s_scr[...] = s0_ref[0, :, 0]

    c = CHUNK
    n2 = 2 * c
    ri = lax.broadcasted_iota(jnp.int32, (n2, n2), 0)
    ci = lax.broadcasted_iota(jnp.int32, (n2, n2), 1)
    same = (ri // c) == (ci // c)
    rp = ri % c
    cp = ci % c
    ti = lax.broadcasted_iota(jnp.int32, (c, c), 0)
    tj = lax.broadcasted_iota(jnp.int32, (c, c), 1)
    m0 = lax.broadcasted_iota(jnp.int32, (c, LANES), 1) < HEAD
    n_chunks = TM // c
    tri_f = jnp.where(tj <= ti, 1.0, 0.0).astype(BF16)
    tri_b = jnp.where(tj >= ti, 1.0, 0.0).astype(BF16)
    strict_f = jnp.logical_and(same, cp < rp)
    incl_f = jnp.logical_and(same, cp <= rp)
    strict_b = jnp.logical_and(same, cp > rp)
    incl_b = jnp.logical_and(same, cp >= rp)

    def body(ch, carry):
        s_f, s_b = carry
        sl = pl.ds(pl.multiple_of(ch * c, c), c)
        o, s_f = _scan_chunk(rf[0, sl, :], vf[0, sl, :], kkf[0, sl, :], lwf[0, sl, :],
                             kf[0, sl, :], bf[0, sl, :], s_f, False, tri_f, strict_f, incl_f, m0)
        of_ref[0, sl, :] = o
        sl = pl.ds(pl.multiple_of((n_chunks - 1 - ch) * c, c), c)
        o, s_b = _scan_chunk(rb[0, sl, :], vb[0, sl, :], kkb[0, sl, :], lwb[0, sl, :],
                             kb[0, sl, :], bb[0, sl, :], s_b, True, tri_b, strict_b, incl_b, m0)
        ob_ref[0, sl, :] = o
        return s_f, s_b

    s_f, s_b = lax.fori_loop(0, n_chunks, body, (s_scr[0], s_scr[1]))
    s_scr[0] = s_f
    s_scr[1] = s_b

    @pl.when(last_ref[it] == 1)
    def _():
        sout_ref[0, 0, 0] = s_f
        sout_ref[0, 1, 0] = s_b


def _rwkv_scan(prep, s0, seq_tiles):
    r, v, kk, lw0, lw1, k0, k1, b0, b1 = prep
    npair, t, _ = r.shape
    ft, bt, sq, first, last = [], [], [], [], []
    for si, (t0, n) in enumerate(seq_tiles):
        for j in range(n):
            ft.append(t0 + j)
            bt.append(t0 + n - 1 - j)
            sq.append(si)
            first.append(int(j == 0))
            last.append(int(j == n - 1))
    tabs = [jnp.asarray(np.array(a, np.int32)) for a in (ft, bt, sq, first, last)]
    fwd = pl.BlockSpec((1, TM, LANES), lambda p, i, ft, bt, sq, fi, la: (p, ft[i], 0))
    bwd = pl.BlockSpec((1, TM, LANES), lambda p, i, ft, bt, sq, fi, la: (p, bt[i], 0))
    st = pl.BlockSpec((1, 2, 1, LANES, LANES), lambda p, i, ft, bt, sq, fi, la: (sq[i], 0, p, 0, 0))
    o_shape = jax.ShapeDtypeStruct((npair, t, LANES), F32)
    grid_spec = pltpu.PrefetchScalarGridSpec(
        num_scalar_prefetch=5,
        grid=(npair, len(ft)),
        in_specs=[fwd] * 6 + [bwd] * 6 + [st],
        out_specs=[fwd, bwd, st],
        scratch_shapes=[pltpu.VMEM((2, LANES, LANES), F32)],
    )
    return pl.pallas_call(
        _rwkv_scan_kernel,
        grid_spec=grid_spec,
        out_shape=[o_shape, o_shape, jax.ShapeDtypeStruct(s0.shape, F32)],
        compiler_params=_cparams(("arbitrary", "arbitrary")),
        name="rwkv_scan",
    )(*tabs, r, v, kk, lw0, k0, b0, r, v, kk, lw1, k1, b1, s0)


def _rwkv_post_kernel(of_ref, ob_ref, g_ref, bv_ref, lnw_ref, lnb_ref, avg_ref, y_ref):
    for p in range(of_ref.shape[0]):
        sl = slice(p * LANES, (p + 1) * LANES)
        o = of_ref[p] + ob_ref[p]
        hi, lo = _split2(o)
        mu = _dot(hi, avg_ref[...]) + _dot(lo, avg_ref[...])
        d = o - mu
        hi, lo = _split2(d * d)
        var = _dot(hi, avg_ref[...]) + _dot(lo, avg_ref[...])
        y = d * lax.rsqrt(var + GN_EPS) * lnw_ref[:, sl] + lnb_ref[:, sl]
        y_ref[:, sl] = ((y + bv_ref[:, sl]) * g_ref[:, sl]).astype(y_ref.dtype)


def _rwkv_post(o_f, o_b, g, bv, lnw, lnb):
    npair, t, _ = o_f.shape
    dr = npair * LANES
    avg = _head_block_ones(1.0 / HEAD)
    pair_spec = pl.BlockSpec((npair, TM, LANES), lambda i: (0, i, 0))
    flat_spec = pl.BlockSpec((TM, dr), lambda i: (i, 0))
    vec = pl.BlockSpec((1, dr), lambda i: (0, 0))
    return pl.pallas_call(
        _rwkv_post_kernel,
        grid=(t // TM,),
        in_specs=[pair_spec, pair_spec, flat_spec, flat_spec, vec, vec,
                  pl.BlockSpec((LANES, LANES), lambda i: (0, 0))],
        out_specs=flat_spec,
        out_shape=jax.ShapeDtypeStruct((t, dr), BF16),
        compiler_params=_cparams(("arbitrary",)),
        name="rwkv_post",
    )(o_f, o_b, g, bv, lnw, lnb, avg)


def _mix_kernel(yf_ref, yr_ref, gf_ref, gr_ref, x_ref, mod_ref, gpost_ref, gpre_ref,
                wf_ref, wr_ref, wo_ref, wrh_ref, wrl_ref, xn_ref, h2_ref, sc_ref):
    a = _dot(yf_ref[...], wf_ref[...])
    b = _dot(yr_ref[...], wr_ref[...])
    merged = jax.nn.sigmoid(gf_ref[...]) * a + jax.nn.sigmoid(gr_ref[...]) * b
    m = _dot(merged.astype(BF16), wo_ref[...])
    mod = mod_ref[0]
    xn = x_ref[...] + mod[2:3] * _rms(m, gpost_ref[...])
    xn_ref[...] = xn
    h2 = _rms(xn, gpre_ref[...]) * (1.0 + mod[4:5]) + mod[3:4]
    h2_ref[...] = h2
    hi, lo = _split2(h2)
    logits = _dot(hi, wrh_ref[...]) + (_dot(hi, wrl_ref[...]) + _dot(lo, wrh_ref[...]))
    sc_ref[...] = jax.nn.sigmoid(logits)


def _mix(yf, yr, proj, gf_col, gr_col, x, mod, gpost, gpre, wf, wr, wo, wrh, wrl, n_ctx_rows, lat_len):
    t, d = x.shape
    tm = 128
    df = yf.shape[1]
    ne = wrh.shape[1]
    row = functools.partial(_seq_row, tile=tm, n_ctx_rows=n_ctx_rows, lat_len=lat_len)
    full = lambda a: pl.BlockSpec(a.shape, lambda i, _n=a.ndim: (0,) * _n)
    rowspec = lambda w, cb=0: pl.BlockSpec((tm, w), lambda i: (i, cb))
    return pl.pallas_call(
        _mix_kernel,
        grid=(t // tm,),
        in_specs=[rowspec(df), rowspec(df), rowspec(d, gf_col // d), rowspec(d, gr_col // d), rowspec(d),
                  pl.BlockSpec((1, 6, d), lambda i: (row(i), 0, 0)),
                  full(gpost), full(gpre), full(wf), full(wr), full(wo), full(wrh), full(wrl)],
        out_specs=[rowspec(d), rowspec(d), rowspec(ne)],
        out_shape=[jax.ShapeDtypeStruct((t, d), F32), jax.ShapeDtypeStruct((t, d), F32),
                   jax.ShapeDtypeStruct((t, ne), F32)],
        compiler_params=_cparams(("arbitrary",)),
        name="mix_out",
    )(yf, yr, proj, proj, x, mod, gpost, gpre, wf, wr, wo, wrh, wrl)


def _route(scores, router_bias):
    t, ne = scores.shape
    per = ne // N_GROUPS
    sel = scores + router_bias[None, :]
    grp = sel.reshape(t, N_GROUPS, per)
    grp_score = jnp.sum(lax.top_k(grp, 2)[0], axis=-1)
    _, gidx = lax.top_k(grp_score, TOPK_GROUPS)
    gmask = jnp.sum(jax.nn.one_hot(gidx, N_GROUPS, dtype=F32), axis=1) > 0
    sel = jnp.where(jnp.repeat(gmask, per, axis=1), sel, -jnp.inf)
    _, eidx = lax.top_k(sel, TOP_K)
    onehot = eidx[:, :, None] == jnp.arange(ne, dtype=eidx.dtype)[None, None, :]
    wts = jnp.sum(jnp.where(onehot, scores[:, None, :], 0.0), axis=-1)
    wts = wts / jnp.sum(wts, axis=-1, keepdims=True) * ROUTED_SCALE
    return eidx.astype(jnp.int32), wts


def _dispatch(eidx, ne):
    t, k = eidx.shape
    a = t * k
    nb = a // MOE_TM + ne
    eid = eidx.reshape(-1)
    counts = jnp.sum((eid[:, None] == jnp.arange(ne, dtype=jnp.int32)[None, :]).astype(jnp.int32), axis=0)
    padc = (-counts) % MOE_TM
    q = jnp.arange(MOE_TM, dtype=jnp.int32)
    pad_valid = q[None, :] < padc[:, None]
    pad_e = jnp.where(pad_valid, jnp.arange(ne, dtype=jnp.int32)[:, None], ne).reshape(-1)
    key_real = eid * (2 * a) + jnp.arange(a, dtype=jnp.int32)
    key_pad = pad_e * (2 * a) + a + jnp.arange(ne * MOE_TM, dtype=jnp.int32) % a
    keys = jnp.sort(jnp.concatenate([key_real, key_pad]))
    slot_e = keys // (2 * a)
    rem = keys % (2 * a)
    is_real = rem < a
    slot_a = jnp.where(is_real, rem, 0)
    slot_tok = slot_a // k
    blk_e = slot_e[::MOE_TM]
    blk_valid = (blk_e < ne).astype(jnp.int32)
    blk_e = jnp.minimum(blk_e, ne - 1).astype(jnp.int32)
    n_slots = nb * MOE_TM
    sort_key = jnp.where(is_real, rem, a)
    _, pos = lax.sort((sort_key, jnp.arange(n_slots, dtype=jnp.int32)), num_keys=1)
    pos = pos[:a].reshape(t, k)
    return slot_tok.astype(jnp.int32), blk_e, blk_valid, pos.astype(jnp.int32), nb


def _expert_kernel(be_ref, bv_ref, tok_ref, tokn_ref, h_hbm, w1_ref, w3_ref, w2_ref, y_ref,
                   xbuf, w1b, w3b, w2b, sem, *, nb):
    i = pl.program_id(0)
    slot = lax.rem(i, 2)

    def issue(tref, s):
        def one(rr, carry):
            pltpu.make_async_copy(h_hbm.at[pl.ds(tref[0, 0, rr], 1), :],
                                  xbuf.at[s, pl.ds(rr, 1), :], sem.at[s]).start()
            return carry
        lax.fori_loop(0, MOE_TM, one, 0)

    @pl.when(jnp.logical_and(i == 0, bv_ref[0] == 1))
    def _():
        issue(tok_ref, 0)

    @pl.when(jnp.logical_and(i + 1 < nb, bv_ref[jnp.minimum(i + 1, nb - 1)] == 1))
    def _():
        issue(tokn_ref, 1 - slot)

    changed = jnp.logical_or(i == 0, be_ref[i] != be_ref[jnp.maximum(i - 1, 0)])

    @pl.when(jnp.logical_and(changed, bv_ref[i] == 1))
    def _():
        w1b[...] = w1_ref[0].astype(BF16)
        w3b[...] = w3_ref[0].astype(BF16)
        w2b[...] = w2_ref[0].astype(BF16)

    @pl.when(bv_ref[i] == 1)
    def _():
        def wait_one(rr, carry):
            pltpu.make_async_copy(h_hbm.at[pl.ds(0, 1), :], xbuf.at[slot, pl.ds(0, 1), :],
                                  sem.at[slot]).wait()
            return carry
        lax.fori_loop(0, MOE_TM, wait_one, 0)
        x = xbuf[slot].astype(BF16)
        a = _dot(x, w1b[...])
        b = _dot(x, w3b[...])
        hb = (a * jax.nn.sigmoid(a) * b).astype(BF16)
        y_ref[...] = _dot(hb, w2b[...])

    @pl.when(bv_ref[i] == 0)
    def _():
        y_ref[...] = jnp.zeros_like(y_ref)


def _experts(h2, slot_tok, blk_e, blk_valid, w1, w3, w2, nb):
    t, d = h2.shape
    ne, _, de = w1.shape
    tok3 = slot_tok.reshape(nb, 1, MOE_TM)
    smem_tok = lambda f: pl.BlockSpec((1, 1, MOE_TM), f, memory_space=pltpu.SMEM)
    grid_spec = pltpu.PrefetchScalarGridSpec(
        num_scalar_prefetch=2,
        grid=(nb,),
        in_specs=[smem_tok(lambda i, be, bv: (i, 0, 0)),
                  smem_tok(lambda i, be, bv: (jnp.minimum(i + 1, nb - 1), 0, 0)),
                  pl.BlockSpec(memory_space=pl.ANY),
                  pl.BlockSpec((1, d, de), lambda i, be, bv: (be[i], 0, 0)),
                  pl.BlockSpec((1, d, de), lambda i, be, bv: (be[i], 0, 0)),
                  pl.BlockSpec((1, de, d), lambda i, be, bv: (be[i], 0, 0))],
        out_specs=pl.BlockSpec((MOE_TM, d), lambda i, be, bv: (i, 0)),
        scratch_shapes=[pltpu.VMEM((2, MOE_TM, d), F32),
                        pltpu.VMEM((d, de), BF16), pltpu.VMEM((d, de), BF16), pltpu.VMEM((de, d), BF16),
                        pltpu.SemaphoreType.DMA((2,))],
    )
    return pl.pallas_call(
        functools.partial(_expert_kernel, nb=nb),
        grid_spec=grid_spec,
        out_shape=jax.ShapeDtypeStruct((nb * MOE_TM, d), F32),
        compiler_params=_cparams(("arbitrary",)),
        name="experts",
    )(blk_e, blk_valid, tok3, tok3, h2, w1, w3, w2)


def _combine_kernel(pos_ref, posn_ref, y_hbm, h2_ref, xn_ref, wt_ref, mod_ref, gpost_ref,
                    w1_ref, w3_ref, w2_ref, o_ref, ybuf, sem, *, n_tiles, k):
    i = pl.program_id(0)
    slot = lax.rem(i, 2)
    n_rows = CMB_TM * k

    def issue(pref, s):
        def one(rr, carry):
            pltpu.make_async_copy(y_hbm.at[pl.ds(pref[0, 0, rr], 1), :],
                                  ybuf.at[s, pl.ds(rr, 1), :], sem.at[s]).start()
            return carry
        lax.fori_loop(0, n_rows, one, 0)

    @pl.when(i == 0)
    def _():
        issue(pos_ref, 0)

    @pl.when(i + 1 < n_tiles)
    def _():
        issue(posn_ref, 1 - slot)

    x = h2_ref[...].astype(BF16)
    a = _dot(x, w1_ref[...])
    b = _dot(x, w3_ref[...])
    f = _dot((a * jax.nn.sigmoid(a) * b).astype(BF16), w2_ref[...])

    def wait_one(rr, carry):
        pltpu.make_async_copy(y_hbm.at[pl.ds(0, 1), :], ybuf.at[slot, pl.ds(0, 1), :], sem.at[slot]).wait()
        return carry
    lax.fori_loop(0, n_rows, wait_one, 0)

    wt = wt_ref[...]
    for kk in range(k):
        f = f + wt[:, kk:kk + 1] * ybuf[slot, kk * CMB_TM:(kk + 1) * CMB_TM, :]
    mod = mod_ref[0]
    o_ref[...] = xn_ref[...] + mod[5:6] * _rms(f, gpost_ref[...])


def _combine(y, pos, wts, h2, xn, mod, gpost, w1, w3, w2, n_ctx_rows, lat_len):
    t, d = h2.shape
    k = pos.shape[1]
    n_tiles = t // CMB_TM
    pos3 = pos.reshape(n_tiles, CMB_TM, k).transpose(0, 2, 1).reshape(n_tiles, 1, CMB_TM * k)
    row = functools.partial(_seq_row, tile=CMB_TM, n_ctx_rows=n_ctx_rows, lat_len=lat_len)
    smem_pos = lambda f: pl.BlockSpec((1, 1, CMB_TM * k), f, memory_space=pltpu.SMEM)
    full = lambda a: pl.BlockSpec(a.shape, lambda i, _n=a.ndim: (0,) * _n)
    rowspec = lambda w: pl.BlockSpec((CMB_TM, w), lambda i: (i, 0))
    return pl.pallas_call(
        functools.partial(_combine_kernel, n_tiles=n_tiles, k=k),
        grid=(n_tiles,),
        in_specs=[smem_pos(lambda i: (i, 0, 0)),
                  smem_pos(lambda i: (jnp.minimum(i + 1, n_tiles - 1), 0, 0)),
                  pl.BlockSpec(memory_space=pl.ANY),
                  rowspec(d), rowspec(d), rowspec(k),
                  pl.BlockSpec((1, 6, d), lambda i: (row(i), 0, 0)),
                  full(gpost), full(w1), full(w3), full(w2)],
        out_specs=rowspec(d),
        out_shape=jax.ShapeDtypeStruct((t, d), F32),
        scratch_shapes=[pltpu.VMEM((2, CMB_TM * k, d), F32), pltpu.SemaphoreType.DMA((2,))],
        compiler_params=_cparams(("arbitrary",)),
        name="combine",
    )(pos3, pos3, y, h2, xn, wts, mod, gpost, w1, w3, w2)


def _pad_cols(w, n):
    return jnp.pad(w, ((0, 0), (0, n - w.shape[1])))


def _pad_rows(w, n):
    return jnp.pad(w, ((0, n - w.shape[0]), (0, 0)))


def _pair_state(s):
    n, nd, h, hv, hk = s.shape
    s = s.reshape(n, nd, h // 2, 2, hv, hk)
    z = jnp.zeros_like(s[:, :, :, 0])
    top = jnp.concatenate([s[:, :, :, 0], z], axis=-1)
    bot = jnp.concatenate([z, s[:, :, :, 1]], axis=-1)
    return jnp.concatenate([top, bot], axis=-2)


def _unpair_state(s):
    n, nd, hp = s.shape[:3]
    a = s[..., :HEAD, :HEAD]
    b = s[..., HEAD:, HEAD:]
    return jnp.stack([a, b], axis=3).reshape(n, nd, hp * 2, HEAD, HEAD)


def kernel(x_prompt, x_sample, state_rwkv, c, c_ctx, w_ada, b_ada, g_pre_mix, g_post_mix, g_pre_ffn, g_post_ffn, w_in, conv_rkv, decay_a, decay_b, decay_bias, iclr_a, iclr_b, iclr_bias, gate_a, gate_b, k_k, k_a, r_k, ln_x_w, ln_x_b, w_fourier_out, w_rwkv_out, w_out, w_router, router_bias, w1_exp, w3_exp, w2_exp, w1_sh, w3_sh, w2_sh):
    n_ctx, l_ctx, d = x_prompt.shape
    n_lat, l_lat, _ = x_sample.shape
    depth = w_ada.shape[0]
    assert l_ctx == TM and l_lat % TM == 0 and l_lat % GRID_W == 0
    n_ctx_rows = n_ctx * l_ctx
    t = n_ctx_rows + n_lat * l_lat
    dr = k_k.shape[1]
    df = w_fourier_out.shape[1]
    n_fg = 4
    gw = df // n_fg
    n_heads = dr // HEAD
    ne = w_router.shape[2]
    lora_r = decay_a.shape[3]
    assert lora_r <= LANES and gate_a.shape[2] == 2 * LANES

    x = jnp.concatenate([x_prompt.reshape(n_ctx_rows, d), x_sample.reshape(n_lat * l_lat, d)], axis=0)
    cond = jnp.concatenate([c_ctx[None, :], c, jnp.zeros((16 - 1 - n_lat, d), F32)], axis=0)
    states = []
    lat_state = state_rwkv
    for l in range(depth):
        mod = _ada(cond, w_ada[l].astype(BF16), b_ada[l][None, :]).reshape(16, 6, d)

        wi = w_in[l]
        lora_w = jnp.concatenate(
            [_pad_cols(decay_a[l, 0], LANES), _pad_cols(decay_a[l, 1], LANES),
             _pad_cols(iclr_a[l, 0], LANES), _pad_cols(iclr_a[l, 1], LANES), gate_a[l]], axis=1)
        lora_w = _pad_cols(lora_w, 1024)
        w_cat = jnp.concatenate([wi[:, df:df + 3 * dr], wi[:, :df], wi[:, df + 3 * dr:], lora_w], axis=1).astype(BF16)
        col_f = 3 * dr
        col_gf = col_f + df
        col_gr = col_gf + d
        col_lora = col_gr + d

        h = _prenorm(x, g_pre_mix[l][None, :], mod, n_ctx_rows, l_lat)
        proj = _matmul(h, w_cat, 512, 1024, F32, "in_proj")

        yf_ctx = _fourier_ctx(proj, col_f, n_ctx, l_ctx, gw, n_fg)
        yf_lat = _fourier_lat(proj, col_f, n_ctx_rows, n_lat, l_lat, gw, n_fg)
        yf = jnp.concatenate([yf_ctx, yf_lat], axis=0)

        db = jnp.stack([_pad_rows(decay_b[l, z], LANES) for z in range(2)]).astype(BF16)
        ib = jnp.stack([_pad_rows(iclr_b[l, z], LANES) for z in range(2)]).astype(BF16)
        prep = _rwkv_prep(proj, col_lora, t, dr, n_ctx_rows, l_lat, conv_rkv[l], db, ib,
                          gate_b[l].astype(BF16), decay_bias[l], iclr_bias[l], k_k[l][None, :],
                          k_a[l][None, :], r_k[l].reshape(1, dr))
        scan_in, gate, bv = prep[:9], prep[9], prep[10]
        s0 = jnp.concatenate([jnp.zeros((n_ctx, 2, n_heads // 2, LANES, LANES), F32),
                              _pair_state(lat_state[:, l])], axis=0)
        per = l_lat // TM
        seq_tiles = [(b, 1) for b in range(n_ctx)] + [(n_ctx + b * per, per) for b in range(n_lat)]
        o_f, o_b, s_fin = _rwkv_scan(scan_in, s0, seq_tiles)
        states.append(_unpair_state(s_fin[:n_ctx]))
        yr = _rwkv_post(o_f, o_b, gate, bv, ln_x_w[l][None, :], ln_x_b[l][None, :])

        wr_hi = w_router[l].astype(BF16)
        wr_lo = (w_router[l] - wr_hi.astype(F32)).astype(BF16)
        xn, h2, scores = _mix(yf, yr, proj, col_gf, col_gr, x, mod, g_post_mix[l][None, :],
                              g_pre_ffn[l][None, :], w_fourier_out[l].astype(BF16),
                              w_rwkv_out[l].astype(BF16), w_out[l].astype(BF16), wr_hi, wr_lo,
                              n_ctx_rows, l_lat)

        eidx, wts = _route(scores, router_bias[l])
        slot_tok, blk_e, blk_valid, pos, nb = _dispatch(eidx, ne)
        y = _experts(h2, slot_tok, blk_e, blk_valid, w1_exp[l], w3_exp[l], w2_exp[l], nb)
        x = _combine(y, pos, wts, h2, xn, mod, g_post_ffn[l][None, :], w1_sh[l].astype(BF16),
                     w3_sh[l].astype(BF16), w2_sh[l].astype(BF16), n_ctx_rows, l_lat)

    y_prompt = x[:n_ctx_rows].reshape(n_ctx, l_ctx, d)
    y_sample = x[n_ctx_rows:].reshape(n_lat, l_lat, d)
    return y_prompt, y_sample, jnp.stack(states, axis=1)
```

```python
import functools

import numpy as np
import jax
import jax.numpy as jnp
from jax import lax
from jax.experimental import pallas as pl
from jax.experimental.pallas import tpu as pltpu

F32 = jnp.float32
BF16 = jnp.bfloat16

HEAD = 64
LANES = 128
TM = 256
CHUNK = 64
SCAN_PAIRS = 4
DECAY_SCALE = 0.606531
GN_EPS = 64e-5
RMS_EPS = 1e-6
N_GROUPS = 8
TOPK_GROUPS = 4
TOP_K = 8
ROUTED_SCALE = 2.5
GRID_W = 64
MOE_TM = 256
CMB_TM = 128
DMA_UNROLL = 16
VMEM_LIMIT = 56 * 1024 * 1024


def _cparams(sem):
    return pltpu.CompilerParams(dimension_semantics=sem, vmem_limit_bytes=VMEM_LIMIT)


def _dot(a, b):
    return jnp.dot(a, b, preferred_element_type=F32)


def _dot_nt(a, b):
    return lax.dot_general(a, b, (((1,), (1,)), ((), ())), preferred_element_type=F32)


def _split2(x):
    hi = x.astype(BF16)
    lo = (x - hi.astype(F32)).astype(BF16)
    return hi, lo


def _rms(x, g):
    return x * lax.rsqrt(jnp.mean(x * x, axis=-1, keepdims=True) + RMS_EPS) * g


def _seq_row(i, tile, n_ctx_rows, lat_len):
    t_ctx = n_ctx_rows // tile
    per = lat_len // tile
    return jnp.where(i < t_ctx, 0, 1 + (i - t_ctx) // per)


def _ada_kernel(c_ref, w_ref, b_ref, o_ref):
    c = c_ref[...]
    s = c * jax.nn.sigmoid(c)
    o_ref[...] = _dot(s.astype(BF16), w_ref[...]) + b_ref[...]


def _ada(cond, w, b):
    m, d = cond.shape
    n = w.shape[1]
    tn = 2048
    return pl.pallas_call(
        _ada_kernel,
        grid=(n // tn,),
        in_specs=[pl.BlockSpec((m, d), lambda j: (0, 0)),
                  pl.BlockSpec((d, tn), lambda j: (0, j)),
                  pl.BlockSpec((1, tn), lambda j: (0, j))],
        out_specs=pl.BlockSpec((m, tn), lambda j: (0, j)),
        out_shape=jax.ShapeDtypeStruct((m, n), F32),
        compiler_params=_cparams(("arbitrary",)),
        name="ada",
    )(cond, w, b)


def _prenorm_kernel(x_ref, g_ref, mod_ref, h_ref):
    m = mod_ref[0]
    y = _rms(x_ref[...], g_ref[...])
    h_ref[...] = (y * (1.0 + m[1:2]) + m[0:1]).astype(h_ref.dtype)


def _prenorm(x, g, mod, n_ctx_rows, lat_len):
    t, d = x.shape
    row = functools.partial(_seq_row, tile=TM, n_ctx_rows=n_ctx_rows, lat_len=lat_len)
    return pl.pallas_call(
        _prenorm_kernel,
        grid=(t // TM,),
        in_specs=[pl.BlockSpec((TM, d), lambda i: (i, 0)),
                  pl.BlockSpec((1, d), lambda i: (0, 0)),
                  pl.BlockSpec((1, 6, d), lambda i: (row(i), 0, 0))],
        out_specs=pl.BlockSpec((TM, d), lambda i: (i, 0)),
        out_shape=jax.ShapeDtypeStruct((t, d), BF16),
        compiler_params=_cparams(("arbitrary",)),
        name="prenorm",
    )(x, g, mod)


def _mm_kernel(x_ref, w_ref, o_ref):
    o_ref[...] = _dot(x_ref[...], w_ref[...]).astype(o_ref.dtype)


def _matmul(x, w, tm, tn, out_dtype, name):
    m, k = x.shape
    n = w.shape[1]
    return pl.pallas_call(
        _mm_kernel,
        grid=(n // tn, m // tm),
        in_specs=[pl.BlockSpec((tm, k), lambda j, i: (i, 0)),
                  pl.BlockSpec((k, tn), lambda j, i: (0, j))],
        out_specs=pl.BlockSpec((tm, tn), lambda j, i: (i, j)),
        out_shape=jax.ShapeDtypeStruct((m, n), out_dtype),
        compiler_params=_cparams(("arbitrary", "arbitrary")),
        name=name,
    )(x, w)


def _dft_mats(n):
    k = np.arange(n)
    ang = 2.0 * np.pi * ((k[:, None] * k[None, :]) % n) / n
    s = 1.0 / np.sqrt(n)
    return np.cos(ang) * s, np.sin(ang) * s


def _fourier_ctx_kernel(z_ref, cs_ref, ls_ref, y_ref):
    gw = z_ref.shape[1]
    ab = _dot(z_ref[...].astype(BF16), cs_ref[...])
    st = jnp.concatenate([ab[:, :gw], ab[:, gw:]], axis=0).astype(BF16)
    y_ref[...] = _dot(ls_ref[...], st).astype(y_ref.dtype)


def _fourier_ctx(proj, col0, n_seq, seq_len, gw, n_groups):
    cc, sc = _dft_mats(gw)
    cl, sl = _dft_mats(seq_len)
    cs = jnp.asarray(np.concatenate([cc, -sc], axis=1), BF16)
    ls = jnp.asarray(np.concatenate([cl, sl], axis=1), BF16)
    cb = col0 // gw
    return pl.pallas_call(
        _fourier_ctx_kernel,
        grid=(n_seq, n_groups),
        in_specs=[pl.BlockSpec((seq_len, gw), lambda b, g: (b, cb + g)),
                  pl.BlockSpec((gw, 2 * gw), lambda b, g: (0, 0)),
                  pl.BlockSpec((seq_len, 2 * seq_len), lambda b, g: (0, 0))],
        out_specs=pl.BlockSpec((seq_len, gw), lambda b, g: (b, g)),
        out_shape=jax.ShapeDtypeStruct((n_seq * seq_len, n_groups * gw), BF16),
        compiler_params=_cparams(("arbitrary", "arbitrary")),
        name="fourier_ctx",
    )(proj, cs, ls)


def _fourier_lat_a_kernel(z_ref, cs_ref, qs_ref, o_ref):
    gw = z_ref.shape[1]
    ab = _dot(z_ref[...].astype(BF16), cs_ref[...])
    for rr in range(TM // GRID_W):
        slab = ab[rr * GRID_W:(rr + 1) * GRID_W]
        swap = jnp.concatenate([slab[:, gw:], -slab[:, :gw]], axis=1)
        st = jnp.concatenate([slab, swap], axis=0).astype(BF16)
        out = _dot(qs_ref[...], st)
        o_ref[0, rr * GRID_W:(rr + 1) * GRID_W, :] = out[:, :gw].astype(o_ref.dtype)
        o_ref[1, rr * GRID_W:(rr + 1) * GRID_W, :] = out[:, gw:].astype(o_ref.dtype)


def _fourier_lat_b_kernel(x_ref, rs_ref, y_ref):
    st = jnp.concatenate([x_ref[0, 0], x_ref[1, 0]], axis=0)
    y_ref[0] = _dot(rs_ref[...], st).astype(y_ref.dtype)


def _fourier_lat(proj, col0, row0, n_seq, seq_len, gw, n_groups):
    rows = seq_len // GRID_W
    cc, sc = _dft_mats(gw)
    cq, sq = _dft_mats(GRID_W)
    cr, sr = _dft_mats(rows)
    cs = jnp.asarray(np.concatenate([cc, -sc], axis=1), BF16)
    qs = jnp.asarray(np.concatenate([cq, sq], axis=1), BF16)
    rs = jnp.asarray(np.concatenate([cr, sr], axis=1), BF16)
    cb = col0 // gw
    rb = row0 // TM
    per = seq_len // TM
    width = n_groups * gw
    ab = pl.pallas_call(
        _fourier_lat_a_kernel,
        grid=(n_seq * per, n_groups),
        in_specs=[pl.BlockSpec((TM, gw), lambda i, g: (rb + i, cb + g)),
                  pl.BlockSpec((gw, 2 * gw), lambda i, g: (0, 0)),
                  pl.BlockSpec((GRID_W, 2 * GRID_W), lambda i, g: (0, 0))],
        out_specs=pl.BlockSpec((2, TM, gw), lambda i, g: (0, i, g)),
        out_shape=jax.ShapeDtypeStruct((2, n_seq * seq_len, width), BF16),
        compiler_params=_cparams(("arbitrary", "arbitrary")),
        name="fourier_lat_cols",
    )(proj, cs, qs)
    flat = GRID_W * width
    tn = min(4096, flat)
    y = pl.pallas_call(
        _fourier_lat_b_kernel,
        grid=(n_seq, flat // tn),
        in_specs=[pl.BlockSpec((2, 1, rows, tn), lambda b, j: (0, b, 0, j)),
                  pl.BlockSpec((rows, 2 * rows), lambda b, j: (0, 0))],
        out_specs=pl.BlockSpec((1, rows, tn), lambda b, j: (b, 0, j)),
        out_shape=jax.ShapeDtypeStruct((n_seq, rows, flat), BF16),
        compiler_params=_cparams(("arbitrary", "arbitrary")),
        name="fourier_lat_rows",
    )(ab.reshape(2, n_seq, rows, flat), rs)
    return y.reshape(n_seq * seq_len, width)


def _group_sum(x, ones_ref):
    hi, lo = _split2(x)
    return _dot(hi, ones_ref[...]) + _dot(lo, ones_ref[...])


def _rwkv_prep_kernel(u_ref, up_ref, un_ref, lora_ref, conv_ref, db_ref, ib_ref, gb_ref,
                      dbias_ref, ibias_ref, kkw_ref, ka_ref, rk_ref, ones_ref, tri_ref,
                      at0_o, rt0_o, bt0_o, kt0_o, pt0_o, at1_o, rt1_o, bt1_o, kt1_o, pt1_o, v_o, g_o, bv_o,
                      *, t_ctx, per):
    i = pl.program_id(0)
    dr = v_o.shape[0] * LANES
    j = i - t_ctx
    is_first = jnp.logical_or(i < t_ctx, lax.rem(j, per) == 0)
    is_last = jnp.logical_or(i < t_ctx, lax.rem(j, per) == per - 1)
    u = u_ref[...]
    rows = lax.broadcasted_iota(jnp.int32, (TM, 1), 0)
    prev_row = jnp.where(is_first, 0.0, up_ref[7:8, :])
    next_row = jnp.where(is_last, 0.0, un_ref[0:1, :])
    u_prev = jnp.where(rows == 0, prev_row, pltpu.roll(u, 1, 0))
    u_next = jnp.where(rows == TM - 1, next_row, pltpu.roll(u, TM - 1, 0))
    cw = conv_ref[...]
    rkv = u_prev * cw[0:1] + u * cw[1:2] + u_next * cw[2:3]
    r = rkv[:, :dr]
    k = rkv[:, dr:2 * dr]
    v = rkv[:, 2 * dr:]

    lora = lora_ref[...]
    dbias = dbias_ref[...]
    ibias = ibias_ref[...]
    lws, avs = [], []
    for z in range(2):
        dz = jnp.tanh(lora[:, z * LANES:(z + 1) * LANES]).astype(BF16)
        lws.append(-DECAY_SCALE * jax.nn.sigmoid(dbias[z:z + 1] + _dot(dz, db_ref[z])))
        iz = lora[:, (2 + z) * LANES:(3 + z) * LANES].astype(BF16)
        avs.append(jax.nn.sigmoid(ibias[z:z + 1] + _dot(iz, ib_ref[z])))
    gate = _dot(jax.nn.sigmoid(lora[:, 4 * LANES:6 * LANES]).astype(BF16), gb_ref[...])

    kkr = k * kkw_ref[...]
    ka = ka_ref[...]
    k0 = k * (1.0 + (avs[0] - 1.0) * ka)
    k1 = k * (1.0 + (avs[1] - 1.0) * ka)
    bon_in = r * (k0 + k1) * rk_ref[...]
    g_o[...] = gate
    kds = (k0, k1)
    outs = ((at0_o, rt0_o, bt0_o, kt0_o, pt0_o), (at1_o, rt1_o, bt1_o, kt1_o, pt1_o))
    n_chunks = TM // CHUNK
    for p in range(dr // LANES):
        sl = slice(p * LANES, (p + 1) * LANES)
        kk_p = kkr[:, sl]
        ss = _group_sum(kk_p * kk_p, ones_ref)
        kk_p = kk_p / jnp.maximum(jnp.sqrt(ss), 1e-12)
        r_p = r[:, sl]
        v_o[p] = v[:, sl].astype(BF16)
        bv_o[:, sl] = _group_sum(bon_in[:, sl], ones_ref) * v[:, sl]
        for z in range(2):
            at_o, rt_o, bt_o, kt_o, pt_o = outs[z]
            lw = lws[z][:, sl]
            h0 = lw.astype(BF16)
            r1 = lw - h0.astype(F32)
            h1 = r1.astype(BF16)
            h2 = (r1 - h1.astype(F32)).astype(BF16)
            cum = _dot(tri_ref[z], h0) + _dot(tri_ref[z], h1) + _dot(tri_ref[z], h2)
            p_in = jnp.exp(cum)
            p_inv = jnp.exp(-cum)
            at_o[p] = (-kk_p * jnp.exp(cum - lw)).astype(BF16)
            rt_o[p] = (r_p * p_in).astype(BF16)
            bt_o[p] = (kk_p * avs[z][:, sl] * p_inv).astype(BF16)
            kt_o[p] = (kds[z][:, sl] * p_inv).astype(BF16)
            for ch in range(n_chunks):
                row = ch * CHUNK + (0 if z == 1 else CHUNK - 1)
                pt_o[p, ch * 8:(ch + 1) * 8, :] = jnp.broadcast_to(p_in[row:row + 1, :], (8, LANES))


def _head_block_ones(scale):
    m = np.zeros((LANES, LANES), np.float32)
    m[:HEAD, :HEAD] = scale
    m[HEAD:, HEAD:] = scale
    return jnp.asarray(m, BF16)


def _rwkv_prep(proj, lora_col, t, dr, n_ctx_rows, lat_len, conv, db, ib, gb, dbias, ibias, kkw, ka, rk):
    t_ctx = n_ctx_rows // TM
    per = lat_len // TM
    n_tiles = t // TM
    npair = dr // LANES
    w3 = 3 * dr
    lb = lora_col // 1024
    nb8 = t // 8
    pair = jax.ShapeDtypeStruct((npair, t, LANES), BF16)
    ptot = jax.ShapeDtypeStruct((npair, t // 8, LANES), F32)
    flat = jax.ShapeDtypeStruct((t, dr), F32)
    full = lambda a: pl.BlockSpec(a.shape, lambda i, _n=a.ndim: (0,) * _n)
    ones = _head_block_ones(1.0)
    idx = np.arange(TM)
    same = (idx[:, None] // CHUNK) == (idx[None, :] // CHUNK)
    tri = jnp.asarray(np.stack([same & (idx[None, :] <= idx[:, None]),
                                same & (idx[None, :] >= idx[:, None])]).astype(np.float32), BF16)
    pair_spec = pl.BlockSpec((npair, TM, LANES), lambda i: (0, i, 0))
    ptot_spec = pl.BlockSpec((npair, TM // 8, LANES), lambda i: (0, i, 0))
    flat_spec = pl.BlockSpec((TM, dr), lambda i: (i, 0))
    dir_specs = [pair_spec] * 4 + [ptot_spec]
    dir_shapes = [pair] * 4 + [ptot]
    return pl.pallas_call(
        functools.partial(_rwkv_prep_kernel, t_ctx=t_ctx, per=per),
        grid=(n_tiles,),
        in_specs=[pl.BlockSpec((TM, w3), lambda i: (i, 0)),
                  pl.BlockSpec((8, w3), lambda i: (jnp.maximum(i * (TM // 8) - 1, 0), 0)),
                  pl.BlockSpec((8, w3), lambda i: (jnp.minimum((i + 1) * (TM // 8), nb8 - 1), 0)),
                  pl.BlockSpec((TM, 1024), lambda i: (i, lb)),
                  full(conv), full(db), full(ib), full(gb), full(dbias), full(ibias),
                  full(kkw), full(ka), full(rk), full(ones), full(tri)],
        out_specs=dir_specs * 2 + [pair_spec, flat_spec, flat_spec],
        out_shape=dir_shapes * 2 + [pair, flat, flat],
        compiler_params=_cparams(("arbitrary",)),
        name="rwkv_prep",
    )(proj, proj, proj, proj, conv, db, ib, gb, dbias, ibias, kkw, ka, rk, ones, tri)


def _scan_chunks(chains, m0):
    c = chains[0][0].shape[0]
    n2 = 2 * c
    zero = jnp.zeros_like(chains[0][0])
    rng = range(len(chains))

    def two_heads(x):
        return jnp.concatenate([jnp.where(m0, x, zero), jnp.where(m0, zero, x)], axis=0)

    vb16 = [two_heads(ch[4]) for ch in chains]
    lhs = [jnp.concatenate([two_heads(ch[0]), two_heads(ch[1])], axis=0) for ch in chains]
    rhs = [jnp.concatenate([two_heads(ch[2]), two_heads(ch[3])], axis=0) for ch in chains]
    gram = [_dot_nt(lhs[i], rhs[i]) for i in rng]
    a_s = [_dot_nt(lhs[i], chains[i][6].astype(BF16)) for i in rng]
    l_ka = [jnp.where(chains[i][7], gram[i][:n2, n2:], 0.0).astype(BF16) for i in rng]
    lp = [jnp.where(chains[i][7], gram[i][:n2, :n2], 0.0).astype(BF16) for i in rng]
    m_rk = [jnp.concatenate([jnp.where(chains[i][8], gram[i][n2:, :n2], 0.0),
                             jnp.where(chains[i][8], gram[i][n2:, n2:], 0.0)], axis=1).astype(BF16)
            for i in rng]
    x = [a_s[i][:n2] + _dot(l_ka[i], vb16[i]) for i in rng]
    n_iter = int(np.log2(c))
    for it in range(n_iter):
        if it < n_iter - 1:
            both = [_dot(lp[i], jnp.concatenate([lp[i], x[i].astype(BF16)], axis=1)) for i in rng]
            lp = [both[i][:, :n2].astype(BF16) for i in rng]
            x = [x[i] + both[i][:, n2:] for i in rng]
        else:
            x = [x[i] + _dot(lp[i], x[i].astype(BF16)) for i in rng]
    o_bd = [a_s[i][n2:] + _dot(m_rk[i], jnp.concatenate([x[i].astype(BF16), vb16[i]], axis=0)) for i in rng]
    uvt = [jnp.concatenate([x[i].T, vb16[i].astype(F32).T], axis=1).astype(BF16) for i in rng]
    s_new = [(chains[i][6] + _dot(uvt[i], rhs[i])) * chains[i][5] for i in rng]
    return [o_bd[i][:c] + o_bd[i][c:] for i in rng], s_new


def _rwkv_scan_kernel(ft_ref, bt_ref, sq_ref, first_ref, last_ref,
                      atf, rtf, btf, ktf, ptf, vf, atb, rtb, btb, ktb, ptb, vb, s0_ref,
                      of_ref, ob_ref, sout_ref, s_scr):
    it = pl.program_id(1)
    npp = s_scr.shape[1]

    @pl.when(first_ref[it] == 1)
    def _():
        s_scr[...] = s0_ref[0]

    c = CHUNK
    n2 = 2 * c
    ri = lax.broadcasted_iota(jnp.int32, (n2, n2), 0)
    ci = lax.broadcasted_iota(jnp.int32, (n2, n2), 1)
    same = (ri // c) == (ci // c)
    rp = ri % c
    cp = ci % c
    m0 = lax.broadcasted_iota(jnp.int32, (c, LANES), 1) < HEAD
    n_chunks = TM // c
    masks = ((jnp.logical_and(same, cp < rp), jnp.logical_and(same, cp <= rp)),
             (jnp.logical_and(same, cp > rp), jnp.logical_and(same, cp >= rp)))
    dirs = ((atf, rtf, btf, ktf, ptf, vf, of_ref), (atb, rtb, btb, ktb, ptb, vb, ob_ref))

    def body(ch, carry):
        chains, dests = [], []
        for q in range(npp):
            for z in range(2):
                a_r, r_r, b_r, k_r, p_r, v_r, o_r = dirs[z]
                cc = ch if z == 0 else n_chunks - 1 - ch
                sl = pl.ds(pl.multiple_of(cc * c, c), c)
                p_tot = p_r[q, pl.ds(pl.multiple_of(cc * 8, 8), 1), :]
                chains.append((a_r[q, sl, :], r_r[q, sl, :], b_r[q, sl, :], k_r[q, sl, :], v_r[q, sl, :],
                               p_tot, s_scr[z, q], masks[z][0], masks[z][1]))
                dests.append((o_r, q, sl, z))
        outs, states = _scan_chunks(chains, m0)
        for (o_r, q, sl, z), o, s_new in zip(dests, outs, states):
            o_r[q, sl, :] = o
            s_scr[z, q] = s_new
        return carry

    lax.fori_loop(0, n_chunks, body, 0)

    @pl.when(last_ref[it] == 1)
    def _():
        sout_ref[0] = s_scr[...]


def _rwkv_scan(prep, s0, seq_tiles):
    at0, rt0, bt0, kt0, pt0, at1, rt1, bt1, kt1, pt1, v = prep
    npair, t, _ = v.shape
    npp = SCAN_PAIRS
    ft, bt, sq, first, last = [], [], [], [], []
    for si, (t0, n) in enumerate(seq_tiles):
        for j in range(n):
            ft.append(t0 + j)
            bt.append(t0 + n - 1 - j)
            sq.append(si)
            first.append(int(j == 0))
            last.append(int(j == n - 1))
    tabs = [jnp.asarray(np.array(a, np.int32)) for a in (ft, bt, sq, first, last)]
    fwd = pl.BlockSpec((npp, TM, LANES), lambda p, i, ft, bt, sq, fi, la: (p, ft[i], 0))
    bwd = pl.BlockSpec((npp, TM, LANES), lambda p, i, ft, bt, sq, fi, la: (p, bt[i], 0))
    fwd_p = pl.BlockSpec((npp, TM // 8, LANES), lambda p, i, ft, bt, sq, fi, la: (p, ft[i], 0))
    bwd_p = pl.BlockSpec((npp, TM // 8, LANES), lambda p, i, ft, bt, sq, fi, la: (p, bt[i], 0))
    st = pl.BlockSpec((1, 2, npp, LANES, LANES), lambda p, i, ft, bt, sq, fi, la: (sq[i], 0, p, 0, 0))
    o_shape = jax.ShapeDtypeStruct((npair, t, LANES), F32)
    grid_spec = pltpu.PrefetchScalarGridSpec(
        num_scalar_prefetch=5,
        grid=(npair // npp, len(ft)),
        in_specs=[fwd] * 4 + [fwd_p, fwd] + [bwd] * 4 + [bwd_p, bwd] + [st],
        out_specs=[fwd, bwd, st],
        scratch_shapes=[pltpu.VMEM((2, npp, LANES, LANES), F32)],
    )
    return pl.pallas_call(
        _rwkv_scan_kernel,
        grid_spec=grid_spec,
        out_shape=[o_shape, o_shape, jax.ShapeDtypeStruct(s0.shape, F32)],
        compiler_params=_cparams(("arbitrary", "arbitrary")),
        name="rwkv_scan",
    )(*tabs, at0, rt0, bt0, kt0, pt0, v, at1, rt1, bt1, kt1, pt1, v, s0)


def _rwkv_post_kernel(of_ref, ob_ref, g_ref, bv_ref, lnw_ref, lnb_ref, avg_ref, y_ref):
    for p in range(of_ref.shape[0]):
        sl = slice(p * LANES, (p + 1) * LANES)
        o = of_ref[p] + ob_ref[p]
        hi, lo = _split2(o)
        mu = _dot(hi, avg_ref[...]) + _dot(lo, avg_ref[...])
        d = o - mu
        hi, lo = _split2(d * d)
        var = _dot(hi, avg_ref[...]) + _dot(lo, avg_ref[...])
        y = d * lax.rsqrt(var + GN_EPS) * lnw_ref[:, sl] + lnb_ref[:, sl]
        y_ref[:, sl] = ((y + bv_ref[:, sl]) * g_ref[:, sl]).astype(y_ref.dtype)


def _rwkv_post(o_f, o_b, g, bv, lnw, lnb):
    npair, t, _ = o_f.shape
    dr = npair * LANES
    avg = _head_block_ones(1.0 / HEAD)
    pair_spec = pl.BlockSpec((npair, TM, LANES), lambda i: (0, i, 0))
    flat_spec = pl.BlockSpec((TM, dr), lambda i: (i, 0))
    vec = pl.BlockSpec((1, dr), lambda i: (0, 0))
    return pl.pallas_call(
        _rwkv_post_kernel,
        grid=(t // TM,),
        in_specs=[pair_spec, pair_spec, flat_spec, flat_spec, vec, vec,
                  pl.BlockSpec((LANES, LANES), lambda i: (0, 0))],
        out_specs=flat_spec,
        out_shape=jax.ShapeDtypeStruct((t, dr), BF16),
        compiler_params=_cparams(("arbitrary",)),
        name="rwkv_post",
    )(o_f, o_b, g, bv, lnw, lnb, avg)


def _mix_kernel(yf_ref, yr_ref, gf_ref, gr_ref, x_ref, mod_ref, gpost_ref, gpre_ref,
                wf_ref, wr_ref, wo_ref, wrh_ref, wrl_ref, xn_ref, h2_ref, sc_ref):
    a = _dot(yf_ref[...], wf_ref[...])
    b = _dot(yr_ref[...], wr_ref[...])
    merged = jax.nn.sigmoid(gf_ref[...]) * a + jax.nn.sigmoid(gr_ref[...]) * b
    m = _dot(merged.astype(BF16), wo_ref[...])
    mod = mod_ref[0]
    xn = x_ref[...] + mod[2:3] * _rms(m, gpost_ref[...])
    xn_ref[...] = xn
    h2 = _rms(xn, gpre_ref[...]) * (1.0 + mod[4:5]) + mod[3:4]
    h2_ref[...] = h2
    hi, lo = _split2(h2)
    logits = _dot(hi, wrh_ref[...]) + (_dot(hi, wrl_ref[...]) + _dot(lo, wrh_ref[...]))
    sc_ref[...] = jax.nn.sigmoid(logits)


def _mix(yf, yr, proj, gf_col, gr_col, x, mod, gpost, gpre, wf, wr, wo, wrh, wrl, n_ctx_rows, lat_len):
    t, d = x.shape
    tm = 128
    df = yf.shape[1]
    ne = wrh.shape[1]
    row = functools.partial(_seq_row, tile=tm, n_ctx_rows=n_ctx_rows, lat_len=lat_len)
    full = lambda a: pl.BlockSpec(a.shape, lambda i, _n=a.ndim: (0,) * _n)
    rowspec = lambda w, cb=0: pl.BlockSpec((tm, w), lambda i: (i, cb))
    return pl.pallas_call(
        _mix_kernel,
        grid=(t // tm,),
        in_specs=[rowspec(df), rowspec(df), rowspec(d, gf_col // d), rowspec(d, gr_col // d), rowspec(d),
                  pl.BlockSpec((1, 6, d), lambda i: (row(i), 0, 0)),
                  full(gpost), full(gpre), full(wf), full(wr), full(wo), full(wrh), full(wrl)],
        out_specs=[rowspec(d), rowspec(d), rowspec(ne)],
        out_shape=[jax.ShapeDtypeStruct((t, d), F32), jax.ShapeDtypeStruct((t, d), F32),
                   jax.ShapeDtypeStruct((t, ne), F32)],
        compiler_params=_cparams(("arbitrary",)),
        name="mix_out",
    )(yf, yr, proj, proj, x, mod, gpost, gpre, wf, wr, wo, wrh, wrl)


def _route_kernel(sc_ref, bias_ref, idx_ref, wt_ref):
    sc = sc_ref[...]
    tm, ne = sc.shape
    per = ne // N_GROUPS
    lane_i = lax.broadcasted_iota(jnp.int32, (tm, ne), 1)
    lane = lane_i.astype(F32)
    gid = lane_i // per
    sel = sc + bias_ref[...]
    neg = -jnp.inf
    far = float(ne)

    def first_max(x):
        m = jnp.max(x, axis=-1, keepdims=True)
        i = jnp.min(jnp.where(x == m, lane, far), axis=-1, keepdims=True)
        return m, i

    gs = []
    for g in range(N_GROUPS):
        xg = jnp.where(gid == g, sel, neg)
        m1, i1 = first_max(xg)
        m2 = jnp.max(jnp.where(lane == i1, neg, xg), axis=-1, keepdims=True)
        gs.append(m1 + m2)
    keep = jnp.zeros((tm, ne), jnp.bool_)
    for g in range(N_GROUPS):
        rank = jnp.zeros((tm, 1), F32)
        for h in range(N_GROUPS):
            if h == g:
                continue
            ahead = (gs[h] >= gs[g]) if h < g else (gs[h] > gs[g])
            rank = rank + jnp.where(ahead, 1.0, 0.0)
        keep = jnp.logical_or(keep, jnp.logical_and(rank < TOPK_GROUPS, gid == g))
    cur = jnp.where(keep, sel, neg)
    lane_k = lax.broadcasted_iota(jnp.int32, (tm, TOP_K), 1)
    idx = jnp.zeros((tm, TOP_K), F32)
    wts = jnp.zeros((tm, TOP_K), F32)
    tot = jnp.zeros((tm, 1), F32)
    for k in range(TOP_K):
        _, ik = first_max(cur)
        hit = lane == ik
        wk = jnp.sum(jnp.where(hit, sc, 0.0), axis=-1, keepdims=True)
        cur = jnp.where(hit, neg, cur)
        idx = jnp.where(lane_k == k, ik, idx)
        wts = jnp.where(lane_k == k, wk, wts)
        tot = tot + wk
    idx_ref[...] = idx.astype(jnp.int32)
    wt_ref[...] = wts / tot * ROUTED_SCALE


def _route(scores, router_bias):
    t, ne = scores.shape
    spec = pl.BlockSpec((TM, TOP_K), lambda i: (i, 0))
    return pl.pallas_call(
        _route_kernel,
        grid=(t // TM,),
        in_specs=[pl.BlockSpec((TM, ne), lambda i: (i, 0)), pl.BlockSpec((1, ne), lambda i: (0, 0))],
        out_specs=[spec, spec],
        out_shape=[jax.ShapeDtypeStruct((t, TOP_K), jnp.int32), jax.ShapeDtypeStruct((t, TOP_K), F32)],
        compiler_params=_cparams(("arbitrary",)),
        name="route",
    )(scores, router_bias[None, :])


def _dispatch(eidx, ne):
    t, k = eidx.shape
    a = t * k
    nb = a // MOE_TM + ne
    eid = eidx.reshape(-1)
    counts = jnp.sum((eid[:, None] == jnp.arange(ne, dtype=jnp.int32)[None, :]).astype(jnp.int32), axis=0)
    padc = (-counts) % MOE_TM
    q = jnp.arange(MOE_TM, dtype=jnp.int32)
    pad_valid = q[None, :] < padc[:, None]
    pad_e = jnp.where(pad_valid, jnp.arange(ne, dtype=jnp.int32)[:, None], ne).reshape(-1)
    key_real = eid * (2 * a) + jnp.arange(a, dtype=jnp.int32)
    key_pad = pad_e * (2 * a) + a + jnp.arange(ne * MOE_TM, dtype=jnp.int32) % a
    keys = jnp.sort(jnp.concatenate([key_real, key_pad]))
    slot_e = keys // (2 * a)
    rem = keys % (2 * a)
    is_real = rem < a
    slot_a = jnp.where(is_real, rem, 0)
    slot_tok = slot_a // k
    blk_e = slot_e[::MOE_TM]
    blk_valid = (blk_e < ne).astype(jnp.int32)
    blk_e = jnp.minimum(blk_e, ne - 1).astype(jnp.int32)
    n_slots = nb * MOE_TM
    sort_key = jnp.where(is_real, rem, a)
    _, pos = lax.sort((sort_key, jnp.arange(n_slots, dtype=jnp.int32)), num_keys=1)
    pos = pos[:a].reshape(t, k)
    return slot_tok.astype(jnp.int32), blk_e, blk_valid, pos.astype(jnp.int32), nb


def _expert_kernel(be_ref, bv_ref, tok_ref, tokn_ref, h_hbm, w1_ref, w3_ref, w2_ref, y_ref,
                   xbuf, w1b, w3b, w2b, sem, *, nb):
    i = pl.program_id(0)
    slot = lax.rem(i, 2)

    def issue(tref, s):
        def batch(bi, carry):
            for u in range(DMA_UNROLL):
                rr = bi * DMA_UNROLL + u
                pltpu.make_async_copy(h_hbm.at[pl.ds(tref[0, 0, rr], 1), :],
                                      xbuf.at[s, pl.ds(rr, 1), :], sem.at[s]).start(priority=u % 2)
            return carry
        lax.fori_loop(0, MOE_TM // DMA_UNROLL, batch, 0)

    @pl.when(jnp.logical_and(i == 0, bv_ref[0] == 1))
    def _():
        issue(tok_ref, 0)

    @pl.when(jnp.logical_and(i + 1 < nb, bv_ref[jnp.minimum(i + 1, nb - 1)] == 1))
    def _():
        issue(tokn_ref, 1 - slot)

    changed = jnp.logical_or(i == 0, be_ref[i] != be_ref[jnp.maximum(i - 1, 0)])

    @pl.when(jnp.logical_and(changed, bv_ref[i] == 1))
    def _():
        w1b[...] = w1_ref[0].astype(BF16)
        w3b[...] = w3_ref[0].astype(BF16)
        w2b[...] = w2_ref[0].astype(BF16)

    @pl.when(bv_ref[i] == 1)
    def _():
        for _ in range(MOE_TM):
            pltpu.make_async_copy(h_hbm.at[pl.ds(0, 1), :], xbuf.at[slot, pl.ds(0, 1), :],
                                  sem.at[slot]).wait()
        x = xbuf[slot].astype(BF16)
        a = _dot(x, w1b[...])
        b = _dot(x, w3b[...])
        hb = (a * jax.nn.sigmoid(a) * b).astype(BF16)
        y_ref[...] = _dot(hb, w2b[...])

    @pl.when(bv_ref[i] == 0)
    def _():
        y_ref[...] = jnp.zeros_like(y_ref)


def _experts(h2, slot_tok, blk_e, blk_valid, w1, w3, w2, nb):
    t, d = h2.shape
    ne, _, de = w1.shape
    tok3 = slot_tok.reshape(nb, 1, MOE_TM)
    smem_tok = lambda f: pl.BlockSpec((1, 1, MOE_TM), f, memory_space=pltpu.SMEM)
    grid_spec = pltpu.PrefetchScalarGridSpec(
        num_scalar_prefetch=2,
        grid=(nb,),
        in_specs=[smem_tok(lambda i, be, bv: (i, 0, 0)),
                  smem_tok(lambda i, be, bv: (jnp.minimum(i + 1, nb - 1), 0, 0)),
                  pl.BlockSpec(memory_space=pl.ANY),
                  pl.BlockSpec((1, d, de), lambda i, be, bv: (be[i], 0, 0)),
                  pl.BlockSpec((1, d, de), lambda i, be, bv: (be[i], 0, 0)),
                  pl.BlockSpec((1, de, d), lambda i, be, bv: (be[i], 0, 0))],
        out_specs=pl.BlockSpec((MOE_TM, d), lambda i, be, bv: (i, 0)),
        scratch_shapes=[pltpu.VMEM((2, MOE_TM, d), F32),
                        pltpu.VMEM((d, de), BF16), pltpu.VMEM((d, de), BF16), pltpu.VMEM((de, d), BF16),
                        pltpu.SemaphoreType.DMA((2,))],
    )
    return pl.pallas_call(
        functools.partial(_expert_kernel, nb=nb),
        grid_spec=grid_spec,
        out_shape=jax.ShapeDtypeStruct((nb * MOE_TM, d), F32),
        compiler_params=_cparams(("arbitrary",)),
        name="experts",
    )(blk_e, blk_valid, tok3, tok3, h2, w1, w3, w2)


def _combine_kernel(pos_ref, posn_ref, y_hbm, h2_ref, xn_ref, wt_ref, mod_ref, gpost_ref,
                    w1_ref, w3_ref, w2_ref, o_ref, ybuf, sem, *, n_tiles, k):
    i = pl.program_id(0)
    slot = lax.rem(i, 2)
    n_rows = CMB_TM * k

    def issue(pref, s):
        def batch(bi, carry):
            for u in range(DMA_UNROLL):
                rr = bi * DMA_UNROLL + u
                pltpu.make_async_copy(y_hbm.at[pl.ds(pref[0, 0, rr], 1), :],
                                      ybuf.at[s, pl.ds(rr, 1), :], sem.at[s]).start(priority=u % 2)
            return carry
        lax.fori_loop(0, n_rows // DMA_UNROLL, batch, 0)

    @pl.when(i == 0)
    def _():
        issue(pos_ref, 0)

    @pl.when(i + 1 < n_tiles)
    def _():
        issue(posn_ref, 1 - slot)

    x = h2_ref[...].astype(BF16)
    a = _dot(x, w1_ref[...])
    b = _dot(x, w3_ref[...])
    f = _dot((a * jax.nn.sigmoid(a) * b).astype(BF16), w2_ref[...])

    for _ in range(n_rows):
        pltpu.make_async_copy(y_hbm.at[pl.ds(0, 1), :], ybuf.at[slot, pl.ds(0, 1), :], sem.at[slot]).wait()

    wt = wt_ref[...]
    for kk in range(k):
        f = f + wt[:, kk:kk + 1] * ybuf[slot, kk * CMB_TM:(kk + 1) * CMB_TM, :]
    mod = mod_ref[0]
    o_ref[...] = xn_ref[...] + mod[5:6] * _rms(f, gpost_ref[...])


def _combine(y, pos, wts, h2, xn, mod, gpost, w1, w3, w2, n_ctx_rows, lat_len):
    t, d = h2.shape
    k = pos.shape[1]
    n_tiles = t // CMB_TM
    pos3 = pos.reshape(n_tiles, CMB_TM, k).transpose(0, 2, 1).reshape(n_tiles, 1, CMB_TM * k)
    row = functools.partial(_seq_row, tile=CMB_TM, n_ctx_rows=n_ctx_rows, lat_len=lat_len)
    smem_pos = lambda f: pl.BlockSpec((1, 1, CMB_TM * k), f, memory_space=pltpu.SMEM)
    full = lambda a: pl.BlockSpec(a.shape, lambda i, _n=a.ndim: (0,) * _n)
    rowspec = lambda w: pl.BlockSpec((CMB_TM, w), lambda i: (i, 0))
    return pl.pallas_call(
        functools.partial(_combine_kernel, n_tiles=n_tiles, k=k),
        grid=(n_tiles,),
        in_specs=[smem_pos(lambda i: (i, 0, 0)),
                  smem_pos(lambda i: (jnp.minimum(i + 1, n_tiles - 1), 0, 0)),
                  pl.BlockSpec(memory_space=pl.ANY),
                  rowspec(d), rowspec(d), rowspec(k),
                  pl.BlockSpec((1, 6, d), lambda i: (row(i), 0, 0)),
                  full(gpost), full(w1), full(w3), full(w2)],
        out_specs=rowspec(d),
        out_shape=jax.ShapeDtypeStruct((t, d), F32),
        scratch_shapes=[pltpu.VMEM((2, CMB_TM * k, d), F32), pltpu.SemaphoreType.DMA((2,))],
        compiler_params=_cparams(("arbitrary",)),
        name="combine",
    )(pos3, pos3, y, h2, xn, wts, mod, gpost, w1, w3, w2)


def _pad_cols(w, n):
    return jnp.pad(w, ((0, 0), (0, n - w.shape[1])))


def _pad_rows(w, n):
    return jnp.pad(w, ((0, n - w.shape[0]), (0, 0)))


def _pair_state(s):
    n, nd, h, hv, hk = s.shape
    s = s.reshape(n, nd, h // 2, 2, hv, hk)
    z = jnp.zeros_like(s[:, :, :, 0])
    top = jnp.concatenate([s[:, :, :, 0], z], axis=-1)
    bot = jnp.concatenate([z, s[:, :, :, 1]], axis=-1)
    return jnp.concatenate([top, bot], axis=-2)


def _unpair_state(s):
    n, nd, hp = s.shape[:3]
    a = s[..., :HEAD, :HEAD]
    b = s[..., HEAD:, HEAD:]
    return jnp.stack([a, b], axis=3).reshape(n, nd, hp * 2, HEAD, HEAD)


def kernel(x_prompt, x_sample, state_rwkv, c, c_ctx, w_ada, b_ada, g_pre_mix, g_post_mix, g_pre_ffn, g_post_ffn, w_in, conv_rkv, decay_a, decay_b, decay_bias, iclr_a, iclr_b, iclr_bias, gate_a, gate_b, k_k, k_a, r_k, ln_x_w, ln_x_b, w_fourier_out, w_rwkv_out, w_out, w_router, router_bias, w1_exp, w3_exp, w2_exp, w1_sh, w3_sh, w2_sh):
    n_ctx, l_ctx, d = x_prompt.shape
    n_lat, l_lat, _ = x_sample.shape
    depth = w_ada.shape[0]
    assert l_ctx == TM and l_lat % TM == 0 and l_lat % GRID_W == 0
    n_ctx_rows = n_ctx * l_ctx
    t = n_ctx_rows + n_lat * l_lat
    dr = k_k.shape[1]
    df = w_fourier_out.shape[1]
    n_fg = 4
    gw = df // n_fg
    n_heads = dr // HEAD
    ne = w_router.shape[2]
    lora_r = decay_a.shape[3]
    assert lora_r <= LANES and gate_a.shape[2] == 2 * LANES

    x = jnp.concatenate([x_prompt.reshape(n_ctx_rows, d), x_sample.reshape(n_lat * l_lat, d)], axis=0)
    cond = jnp.concatenate([c_ctx[None, :], c, jnp.zeros((16 - 1 - n_lat, d), F32)], axis=0)
    states = []
    lat_state = state_rwkv
    for l in range(depth):
        mod = _ada(cond, w_ada[l].astype(BF16), b_ada[l][None, :]).reshape(16, 6, d)

        wi = w_in[l]
        lora_w = jnp.concatenate(
            [_pad_cols(decay_a[l, 0], LANES), _pad_cols(decay_a[l, 1], LANES),
             _pad_cols(iclr_a[l, 0], LANES), _pad_cols(iclr_a[l, 1], LANES), gate_a[l]], axis=1)
        lora_w = _pad_cols(lora_w, 1024)
        w_cat = jnp.concatenate([wi[:, df:df + 3 * dr], wi[:, :df], wi[:, df + 3 * dr:], lora_w], axis=1).astype(BF16)
        col_f = 3 * dr
        col_gf = col_f + df
        col_gr = col_gf + d
        col_lora = col_gr + d

        h = _prenorm(x, g_pre_mix[l][None, :], mod, n_ctx_rows, l_lat)
        proj = _matmul(h, w_cat, 512, 1024, F32, "in_proj")

        yf_ctx = _fourier_ctx(proj, col_f, n_ctx, l_ctx, gw, n_fg)
        yf_lat = _fourier_lat(proj, col_f, n_ctx_rows, n_lat, l_lat, gw, n_fg)
        yf = jnp.concatenate([yf_ctx, yf_lat], axis=0)

        db = jnp.stack([_pad_rows(decay_b[l, z], LANES) for z in range(2)]).astype(BF16)
        ib = jnp.stack([_pad_rows(iclr_b[l, z], LANES) for z in range(2)]).astype(BF16)
        prep = _rwkv_prep(proj, col_lora, t, dr, n_ctx_rows, l_lat, conv_rkv[l], db, ib,
                          gate_b[l].astype(BF16), decay_bias[l], iclr_bias[l], k_k[l][None, :],
                          k_a[l][None, :], r_k[l].reshape(1, dr))
        scan_in, gate, bv = prep[:11], prep[11], prep[12]
        s0 = jnp.concatenate([jnp.zeros((n_ctx, 2, n_heads // 2, LANES, LANES), F32),
                              _pair_state(lat_state[:, l])], axis=0)
        per = l_lat // TM
        seq_tiles = [(b, 1) for b in range(n_ctx)] + [(n_ctx + b * per, per) for b in range(n_lat)]
        o_f, o_b, s_fin = _rwkv_scan(scan_in, s0, seq_tiles)
        states.append(_unpair_state(s_fin[:n_ctx]))
        yr = _rwkv_post(o_f, o_b, gate, bv, ln_x_w[l][None, :], ln_x_b[l][None, :])

        wr_hi = w_router[l].astype(BF16)
        wr_lo = (w_router[l] - wr_hi.astype(F32)).astype(BF16)
        xn, h2, scores = _mix(yf, yr, proj, col_gf, col_gr, x, mod, g_post_mix[l][None, :],
                              g_pre_ffn[l][None, :], w_fourier_out[l].astype(BF16),
                              w_rwkv_out[l].astype(BF16), w_out[l].astype(BF16), wr_hi, wr_lo,
                              n_ctx_rows, l_lat)

        eidx, wts = _route(scores, router_bias[l])
        slot_tok, blk_e, blk_valid, pos, nb = _dispatch(eidx, ne)
        y = _experts(h2, slot_tok, blk_e, blk_valid, w1_exp[l], w3_exp[l], w2_exp[l], nb)
        x = _combine(y, pos, wts, h2, xn, mod, g_post_ffn[l][None, :], w1_sh[l].astype(BF16),
                     w3_sh[l].astype(BF16), w2_sh[l].astype(BF16), n_ctx_rows, l_lat)

    y_prompt = x[:n_ctx_rows].reshape(n_ctx, l_ctx, d)
    y_sample = x[n_ctx_rows:].reshape(n_lat, l_lat, d)
    return y_prompt, y_sample, jnp.stack(states, axis=1)
```

```python
import functools

import numpy as np
import jax
import jax.numpy as jnp
from jax import lax
from jax.experimental import pallas as pl
from jax.experimental.pallas import tpu as pltpu

F32 = jnp.float32
BF16 = jnp.bfloat16

HEAD = 64
LANES = 128
TM = 256
CHUNK = 64
SCAN_PAIRS = 4
DECAY_SCALE = 0.606531
GN_EPS = 64e-5
RMS_EPS = 1e-6
N_GROUPS = 8
TOPK_GROUPS = 4
TOP_K = 8
ROUTED_SCALE = 2.5
GRID_W = 64
MOE_TM = 256
CMB_TM = 128
DMA_UNROLL = 16
VMEM_LIMIT = 56 * 1024 * 1024


def _cparams(sem):
    return pltpu.CompilerParams(dimension_semantics=sem, vmem_limit_bytes=VMEM_LIMIT)


def _dot(a, b):
    return jnp.dot(a, b, preferred_element_type=F32)


def _dot_nt(a, b):
    return lax.dot_general(a, b, (((1,), (1,)), ((), ())), preferred_element_type=F32)


def _split2(x):
    hi = x.astype(BF16)
    lo = (x - hi.astype(F32)).astype(BF16)
    return hi, lo


def _rms(x, g):
    return x * lax.rsqrt(jnp.mean(x * x, axis=-1, keepdims=True) + RMS_EPS) * g


def _pack_halves(x):
    n = x.shape[1] // 2
    bits = lax.bitcast_convert_type(x.astype(BF16).astype(F32), jnp.uint32)
    return (bits[:, n:] & jnp.uint32(0xFFFF0000)) | (bits[:, :n] >> 16)


def _unpack_halves(p):
    lo = lax.bitcast_convert_type(p << 16, F32)
    hi = lax.bitcast_convert_type(p & jnp.uint32(0xFFFF0000), F32)
    return lo, hi


def _seq_row(i, tile, n_ctx_rows, lat_len):
    t_ctx = n_ctx_rows // tile
    per = lat_len // tile
    return jnp.where(i < t_ctx, 0, 1 + (i - t_ctx) // per)


def _ada_kernel(c_ref, w_ref, b_ref, o_ref):
    c = c_ref[...]
    s = c * jax.nn.sigmoid(c)
    o_ref[...] = _dot(s.astype(BF16), w_ref[...]) + b_ref[...]


def _ada(cond, w, b):
    m, d = cond.shape
    n = w.shape[1]
    tn = 2048
    return pl.pallas_call(
        _ada_kernel,
        grid=(n // tn,),
        in_specs=[pl.BlockSpec((m, d), lambda j: (0, 0)),
                  pl.BlockSpec((d, tn), lambda j: (0, j)),
                  pl.BlockSpec((1, tn), lambda j: (0, j))],
        out_specs=pl.BlockSpec((m, tn), lambda j: (0, j)),
        out_shape=jax.ShapeDtypeStruct((m, n), F32),
        compiler_params=_cparams(("arbitrary",)),
        name="ada",
    )(cond, w, b)


def _prenorm_kernel(x_ref, g_ref, mod_ref, h_ref):
    m = mod_ref[0]
    y = _rms(x_ref[...], g_ref[...])
    h_ref[...] = (y * (1.0 + m[1:2]) + m[0:1]).astype(h_ref.dtype)


def _prenorm(x, g, mod, n_ctx_rows, lat_len):
    t, d = x.shape
    row = functools.partial(_seq_row, tile=TM, n_ctx_rows=n_ctx_rows, lat_len=lat_len)
    return pl.pallas_call(
        _prenorm_kernel,
        grid=(t // TM,),
        in_specs=[pl.BlockSpec((TM, d), lambda i: (i, 0)),
                  pl.BlockSpec((1, d), lambda i: (0, 0)),
                  pl.BlockSpec((1, 6, d), lambda i: (row(i), 0, 0))],
        out_specs=pl.BlockSpec((TM, d), lambda i: (i, 0)),
        out_shape=jax.ShapeDtypeStruct((t, d), BF16),
        compiler_params=_cparams(("arbitrary",)),
        name="prenorm",
    )(x, g, mod)


def _mm_kernel(x_ref, w_ref, o_ref):
    o_ref[...] = _dot(x_ref[...], w_ref[...]).astype(o_ref.dtype)


def _matmul(x, w, tm, tn, out_dtype, name):
    m, k = x.shape
    n = w.shape[1]
    return pl.pallas_call(
        _mm_kernel,
        grid=(n // tn, m // tm),
        in_specs=[pl.BlockSpec((tm, k), lambda j, i: (i, 0)),
                  pl.BlockSpec((k, tn), lambda j, i: (0, j))],
        out_specs=pl.BlockSpec((tm, tn), lambda j, i: (i, j)),
        out_shape=jax.ShapeDtypeStruct((m, n), out_dtype),
        compiler_params=_cparams(("arbitrary", "arbitrary")),
        name=name,
    )(x, w)


def _dft_mats(n):
    k = np.arange(n)
    ang = 2.0 * np.pi * ((k[:, None] * k[None, :]) % n) / n
    s = 1.0 / np.sqrt(n)
    return np.cos(ang) * s, np.sin(ang) * s


def _fourier_ctx_kernel(z_ref, cs_ref, ls_ref, y_ref):
    gw = z_ref.shape[1]
    ab = _dot(z_ref[...].astype(BF16), cs_ref[...])
    st = jnp.concatenate([ab[:, :gw], ab[:, gw:]], axis=0).astype(BF16)
    y_ref[...] = _dot(ls_ref[...], st).astype(y_ref.dtype)


def _fourier_ctx(proj, col0, n_seq, seq_len, gw, n_groups):
    cc, sc = _dft_mats(gw)
    cl, sl = _dft_mats(seq_len)
    cs = jnp.asarray(np.concatenate([cc, -sc], axis=1), BF16)
    ls = jnp.asarray(np.concatenate([cl, sl], axis=1), BF16)
    cb = col0 // gw
    return pl.pallas_call(
        _fourier_ctx_kernel,
        grid=(n_seq, n_groups),
        in_specs=[pl.BlockSpec((seq_len, gw), lambda b, g: (b, cb + g)),
                  pl.BlockSpec((gw, 2 * gw), lambda b, g: (0, 0)),
                  pl.BlockSpec((seq_len, 2 * seq_len), lambda b, g: (0, 0))],
        out_specs=pl.BlockSpec((seq_len, gw), lambda b, g: (b, g)),
        out_shape=jax.ShapeDtypeStruct((n_seq * seq_len, n_groups * gw), BF16),
        compiler_params=_cparams(("arbitrary", "arbitrary")),
        name="fourier_ctx",
    )(proj, cs, ls)


def _fourier_lat_a_kernel(z_ref, cs_ref, qs_ref, o_ref):
    gw = z_ref.shape[1]
    ab = _dot(z_ref[...].astype(BF16), cs_ref[...])
    for rr in range(TM // GRID_W):
        slab = ab[rr * GRID_W:(rr + 1) * GRID_W]
        swap = jnp.concatenate([slab[:, gw:], -slab[:, :gw]], axis=1)
        st = jnp.concatenate([slab, swap], axis=0).astype(BF16)
        out = _dot(qs_ref[...], st)
        o_ref[0, rr * GRID_W:(rr + 1) * GRID_W, :] = out[:, :gw].astype(o_ref.dtype)
        o_ref[1, rr * GRID_W:(rr + 1) * GRID_W, :] = out[:, gw:].astype(o_ref.dtype)


def _fourier_lat_b_kernel(x_ref, rs_ref, y_ref):
    st = jnp.concatenate([x_ref[0, 0], x_ref[1, 0]], axis=0)
    y_ref[0] = _dot(rs_ref[...], st).astype(y_ref.dtype)


def _fourier_lat(proj, col0, row0, n_seq, seq_len, gw, n_groups):
    rows = seq_len // GRID_W
    cc, sc = _dft_mats(gw)
    cq, sq = _dft_mats(GRID_W)
    cr, sr = _dft_mats(rows)
    cs = jnp.asarray(np.concatenate([cc, -sc], axis=1), BF16)
    qs = jnp.asarray(np.concatenate([cq, sq], axis=1), BF16)
    rs = jnp.asarray(np.concatenate([cr, sr], axis=1), BF16)
    cb = col0 // gw
    rb = row0 // TM
    per = seq_len // TM
    width = n_groups * gw
    ab = pl.pallas_call(
        _fourier_lat_a_kernel,
        grid=(n_seq * per, n_groups),
        in_specs=[pl.BlockSpec((TM, gw), lambda i, g: (rb + i, cb + g)),
                  pl.BlockSpec((gw, 2 * gw), lambda i, g: (0, 0)),
                  pl.BlockSpec((GRID_W, 2 * GRID_W), lambda i, g: (0, 0))],
        out_specs=pl.BlockSpec((2, TM, gw), lambda i, g: (0, i, g)),
        out_shape=jax.ShapeDtypeStruct((2, n_seq * seq_len, width), BF16),
        compiler_params=_cparams(("arbitrary", "arbitrary")),
        name="fourier_lat_cols",
    )(proj, cs, qs)
    flat = GRID_W * width
    tn = min(4096, flat)
    y = pl.pallas_call(
        _fourier_lat_b_kernel,
        grid=(n_seq, flat // tn),
        in_specs=[pl.BlockSpec((2, 1, rows, tn), lambda b, j: (0, b, 0, j)),
                  pl.BlockSpec((rows, 2 * rows), lambda b, j: (0, 0))],
        out_specs=pl.BlockSpec((1, rows, tn), lambda b, j: (b, 0, j)),
        out_shape=jax.ShapeDtypeStruct((n_seq, rows, flat), BF16),
        compiler_params=_cparams(("arbitrary", "arbitrary")),
        name="fourier_lat_rows",
    )(ab.reshape(2, n_seq, rows, flat), rs)
    return y.reshape(n_seq * seq_len, width)


def _group_sum(x, ones_ref):
    hi, lo = _split2(x)
    return _dot(hi, ones_ref[...]) + _dot(lo, ones_ref[...])


def _rwkv_prep_kernel(u_ref, up_ref, un_ref, lora_ref, conv_ref, db_ref, ib_ref, gb_ref,
                      dbias_ref, ibias_ref, kkw_ref, ka_ref, rk_ref, ones_ref, tri_ref,
                      at0_o, rt0_o, bt0_o, kt0_o, pt0_o, at1_o, rt1_o, bt1_o, kt1_o, pt1_o, v_o, g_o, bv_o,
                      *, t_ctx, per):
    i = pl.program_id(0)
    dr = v_o.shape[0] * LANES
    j = i - t_ctx
    is_first = jnp.logical_or(i < t_ctx, lax.rem(j, per) == 0)
    is_last = jnp.logical_or(i < t_ctx, lax.rem(j, per) == per - 1)
    u = u_ref[...]
    rows = lax.broadcasted_iota(jnp.int32, (TM, 1), 0)
    prev_row = jnp.where(is_first, 0.0, up_ref[7:8, :])
    next_row = jnp.where(is_last, 0.0, un_ref[0:1, :])
    u_prev = jnp.where(rows == 0, prev_row, pltpu.roll(u, 1, 0))
    u_next = jnp.where(rows == TM - 1, next_row, pltpu.roll(u, TM - 1, 0))
    cw = conv_ref[...]
    rkv = u_prev * cw[0:1] + u * cw[1:2] + u_next * cw[2:3]
    r = rkv[:, :dr]
    k = rkv[:, dr:2 * dr]
    v = rkv[:, 2 * dr:]

    lora = lora_ref[...]
    dbias = dbias_ref[...]
    ibias = ibias_ref[...]
    lws, avs = [], []
    for z in range(2):
        dz = jnp.tanh(lora[:, z * LANES:(z + 1) * LANES]).astype(BF16)
        lws.append(-DECAY_SCALE * jax.nn.sigmoid(dbias[z:z + 1] + _dot(dz, db_ref[z])))
        iz = lora[:, (2 + z) * LANES:(3 + z) * LANES].astype(BF16)
        avs.append(jax.nn.sigmoid(ibias[z:z + 1] + _dot(iz, ib_ref[z])))
    gate = _dot(jax.nn.sigmoid(lora[:, 4 * LANES:6 * LANES]).astype(BF16), gb_ref[...])

    kkr = k * kkw_ref[...]
    ka = ka_ref[...]
    k0 = k * (1.0 + (avs[0] - 1.0) * ka)
    k1 = k * (1.0 + (avs[1] - 1.0) * ka)
    bon_in = r * (k0 + k1) * rk_ref[...]
    g_o[...] = gate
    kds = (k0, k1)
    outs = ((at0_o, rt0_o, bt0_o, kt0_o, pt0_o), (at1_o, rt1_o, bt1_o, kt1_o, pt1_o))
    n_chunks = TM // CHUNK
    for p in range(dr // LANES):
        sl = slice(p * LANES, (p + 1) * LANES)
        kk_p = kkr[:, sl]
        ss = _group_sum(kk_p * kk_p, ones_ref)
        kk_p = kk_p / jnp.maximum(jnp.sqrt(ss), 1e-12)
        r_p = r[:, sl]
        v_o[p] = v[:, sl].astype(BF16)
        bv_o[:, sl] = _group_sum(bon_in[:, sl], ones_ref) * v[:, sl]
        for z in range(2):
            at_o, rt_o, bt_o, kt_o, pt_o = outs[z]
            lw = lws[z][:, sl]
            h0 = lw.astype(BF16)
            r1 = lw - h0.astype(F32)
            h1 = r1.astype(BF16)
            h2 = (r1 - h1.astype(F32)).astype(BF16)
            cum = _dot(tri_ref[z], h0) + _dot(tri_ref[z], h1) + _dot(tri_ref[z], h2)
            p_in = jnp.exp(cum)
            p_inv = jnp.exp(-cum)
            at_o[p] = (-kk_p * jnp.exp(cum - lw)).astype(BF16)
            rt_o[p] = (r_p * p_in).astype(BF16)
            bt_o[p] = (kk_p * avs[z][:, sl] * p_inv).astype(BF16)
            kt_o[p] = (kds[z][:, sl] * p_inv).astype(BF16)
            for ch in range(n_chunks):
                row = ch * CHUNK + (0 if z == 1 else CHUNK - 1)
                pt_o[p, ch * 8:(ch + 1) * 8, :] = jnp.broadcast_to(p_in[row:row + 1, :], (8, LANES))


def _head_block_ones(scale):
    m = np.zeros((LANES, LANES), np.float32)
    m[:HEAD, :HEAD] = scale
    m[HEAD:, HEAD:] = scale
    return jnp.asarray(m, BF16)


def _rwkv_prep(proj, lora_col, t, dr, n_ctx_rows, lat_len, conv, db, ib, gb, dbias, ibias, kkw, ka, rk):
    t_ctx = n_ctx_rows // TM
    per = lat_len // TM
    n_tiles = t // TM
    npair = dr // LANES
    w3 = 3 * dr
    lb = lora_col // 1024
    nb8 = t // 8
    pair = jax.ShapeDtypeStruct((npair, t, LANES), BF16)
    ptot = jax.ShapeDtypeStruct((npair, t // 8, LANES), F32)
    flat = jax.ShapeDtypeStruct((t, dr), F32)
    full = lambda a: pl.BlockSpec(a.shape, lambda i, _n=a.ndim: (0,) * _n)
    ones = _head_block_ones(1.0)
    idx = np.arange(TM)
    same = (idx[:, None] // CHUNK) == (idx[None, :] // CHUNK)
    tri = jnp.asarray(np.stack([same & (idx[None, :] <= idx[:, None]),
                                same & (idx[None, :] >= idx[:, None])]).astype(np.float32), BF16)
    pair_spec = pl.BlockSpec((npair, TM, LANES), lambda i: (0, i, 0))
    ptot_spec = pl.BlockSpec((npair, TM // 8, LANES), lambda i: (0, i, 0))
    flat_spec = pl.BlockSpec((TM, dr), lambda i: (i, 0))
    dir_specs = [pair_spec] * 4 + [ptot_spec]
    dir_shapes = [pair] * 4 + [ptot]
    return pl.pallas_call(
        functools.partial(_rwkv_prep_kernel, t_ctx=t_ctx, per=per),
        grid=(n_tiles,),
        in_specs=[pl.BlockSpec((TM, w3), lambda i: (i, 0)),
                  pl.BlockSpec((8, w3), lambda i: (jnp.maximum(i * (TM // 8) - 1, 0), 0)),
                  pl.BlockSpec((8, w3), lambda i: (jnp.minimum((i + 1) * (TM // 8), nb8 - 1), 0)),
                  pl.BlockSpec((TM, 1024), lambda i: (i, lb)),
                  full(conv), full(db), full(ib), full(gb), full(dbias), full(ibias),
                  full(kkw), full(ka), full(rk), full(ones), full(tri)],
        out_specs=dir_specs * 2 + [pair_spec, flat_spec, flat_spec],
        out_shape=dir_shapes * 2 + [pair, flat, flat],
        compiler_params=_cparams(("arbitrary",)),
        name="rwkv_prep",
    )(proj, proj, proj, proj, conv, db, ib, gb, dbias, ibias, kkw, ka, rk, ones, tri)


def _scan_chunks(chains, m0):
    c = chains[0][0].shape[0]
    n2 = 2 * c
    zero = jnp.zeros_like(chains[0][0])
    rng = range(len(chains))

    def two_heads(x):
        return jnp.concatenate([jnp.where(m0, x, zero), jnp.where(m0, zero, x)], axis=0)

    vb16 = [two_heads(ch[4]) for ch in chains]
    lhs = [jnp.concatenate([two_heads(ch[0]), two_heads(ch[1])], axis=0) for ch in chains]
    rhs = [jnp.concatenate([two_heads(ch[2]), two_heads(ch[3])], axis=0) for ch in chains]
    gram = [_dot_nt(lhs[i], rhs[i]) for i in rng]
    a_s = [_dot_nt(lhs[i], chains[i][6].astype(BF16)) for i in rng]
    l_ka = [jnp.where(chains[i][7], gram[i][:n2, n2:], 0.0).astype(BF16) for i in rng]
    lp = [jnp.where(chains[i][7], gram[i][:n2, :n2], 0.0).astype(BF16) for i in rng]
    m_rk = [jnp.concatenate([jnp.where(chains[i][8], gram[i][n2:, :n2], 0.0),
                             jnp.where(chains[i][8], gram[i][n2:, n2:], 0.0)], axis=1).astype(BF16)
            for i in rng]
    x = [a_s[i][:n2] + _dot(l_ka[i], vb16[i]) for i in rng]
    n_iter = int(np.log2(c))
    for it in range(n_iter):
        if it < n_iter - 1:
            both = [_dot(lp[i], jnp.concatenate([lp[i], x[i].astype(BF16)], axis=1)) for i in rng]
            lp = [both[i][:, :n2].astype(BF16) for i in rng]
            x = [x[i] + both[i][:, n2:] for i in rng]
        else:
            x = [x[i] + _dot(lp[i], x[i].astype(BF16)) for i in rng]
    o_bd = [a_s[i][n2:] + _dot(m_rk[i], jnp.concatenate([x[i].astype(BF16), vb16[i]], axis=0)) for i in rng]
    uvt = [jnp.concatenate([x[i].T, vb16[i].astype(F32).T], axis=1).astype(BF16) for i in rng]
    s_new = [(chains[i][6] + _dot(uvt[i], rhs[i])) * chains[i][5] for i in rng]
    return [o_bd[i][:c] + o_bd[i][c:] for i in rng], s_new


def _rwkv_scan_kernel(ft_ref, bt_ref, sq_ref, first_ref, last_ref,
                      atf, rtf, btf, ktf, ptf, vf, atb, rtb, btb, ktb, ptb, vb, s0_ref,
                      of_ref, ob_ref, sout_ref, s_scr):
    it = pl.program_id(1)
    npp = s_scr.shape[1]

    @pl.when(first_ref[it] == 1)
    def _():
        s_scr[...] = s0_ref[0]

    c = CHUNK
    n2 = 2 * c
    ri = lax.broadcasted_iota(jnp.int32, (n2, n2), 0)
    ci = lax.broadcasted_iota(jnp.int32, (n2, n2), 1)
    same = (ri // c) == (ci // c)
    rp = ri % c
    cp = ci % c
    m0 = lax.broadcasted_iota(jnp.int32, (c, LANES), 1) < HEAD
    n_chunks = TM // c
    masks = ((jnp.logical_and(same, cp < rp), jnp.logical_and(same, cp <= rp)),
             (jnp.logical_and(same, cp > rp), jnp.logical_and(same, cp >= rp)))
    dirs = ((atf, rtf, btf, ktf, ptf, vf, of_ref), (atb, rtb, btb, ktb, ptb, vb, ob_ref))

    def body(ch, carry):
        chains, dests = [], []
        for q in range(npp):
            for z in range(2):
                a_r, r_r, b_r, k_r, p_r, v_r, o_r = dirs[z]
                cc = ch if z == 0 else n_chunks - 1 - ch
                sl = pl.ds(pl.multiple_of(cc * c, c), c)
                p_tot = p_r[q, pl.ds(pl.multiple_of(cc * 8, 8), 1), :]
                chains.append((a_r[q, sl, :], r_r[q, sl, :], b_r[q, sl, :], k_r[q, sl, :], v_r[q, sl, :],
                               p_tot, s_scr[z, q], masks[z][0], masks[z][1]))
                dests.append((o_r, q, sl, z))
        outs, states = _scan_chunks(chains, m0)
        for (o_r, q, sl, z), o, s_new in zip(dests, outs, states):
            o_r[q, sl, :] = o
            s_scr[z, q] = s_new
        return carry

    lax.fori_loop(0, n_chunks, body, 0)

    @pl.when(last_ref[it] == 1)
    def _():
        sout_ref[0] = s_scr[...]


def _rwkv_scan(prep, s0, seq_tiles):
    at0, rt0, bt0, kt0, pt0, at1, rt1, bt1, kt1, pt1, v = prep
    npair, t, _ = v.shape
    npp = SCAN_PAIRS
    ft, bt, sq, first, last = [], [], [], [], []
    for si, (t0, n) in enumerate(seq_tiles):
        for j in range(n):
            ft.append(t0 + j)
            bt.append(t0 + n - 1 - j)
            sq.append(si)
            first.append(int(j == 0))
            last.append(int(j == n - 1))
    tabs = [jnp.asarray(np.array(a, np.int32)) for a in (ft, bt, sq, first, last)]
    fwd = pl.BlockSpec((npp, TM, LANES), lambda p, i, ft, bt, sq, fi, la: (p, ft[i], 0))
    bwd = pl.BlockSpec((npp, TM, LANES), lambda p, i, ft, bt, sq, fi, la: (p, bt[i], 0))
    fwd_p = pl.BlockSpec((npp, TM // 8, LANES), lambda p, i, ft, bt, sq, fi, la: (p, ft[i], 0))
    bwd_p = pl.BlockSpec((npp, TM // 8, LANES), lambda p, i, ft, bt, sq, fi, la: (p, bt[i], 0))
    st = pl.BlockSpec((1, 2, npp, LANES, LANES), lambda p, i, ft, bt, sq, fi, la: (sq[i], 0, p, 0, 0))
    o_shape = jax.ShapeDtypeStruct((npair, t, LANES), F32)
    grid_spec = pltpu.PrefetchScalarGridSpec(
        num_scalar_prefetch=5,
        grid=(npair // npp, len(ft)),
        in_specs=[fwd] * 4 + [fwd_p, fwd] + [bwd] * 4 + [bwd_p, bwd] + [st],
        out_specs=[fwd, bwd, st],
        scratch_shapes=[pltpu.VMEM((2, npp, LANES, LANES), F32)],
    )
    return pl.pallas_call(
        _rwkv_scan_kernel,
        grid_spec=grid_spec,
        out_shape=[o_shape, o_shape, jax.ShapeDtypeStruct(s0.shape, F32)],
        compiler_params=_cparams(("arbitrary", "arbitrary")),
        name="rwkv_scan",
    )(*tabs, at0, rt0, bt0, kt0, pt0, v, at1, rt1, bt1, kt1, pt1, v, s0)


def _rwkv_post_kernel(of_ref, ob_ref, g_ref, bv_ref, lnw_ref, lnb_ref, avg_ref, y_ref):
    for p in range(of_ref.shape[0]):
        sl = slice(p * LANES, (p + 1) * LANES)
        o = of_ref[p] + ob_ref[p]
        hi, lo = _split2(o)
        mu = _dot(hi, avg_ref[...]) + _dot(lo, avg_ref[...])
        d = o - mu
        hi, lo = _split2(d * d)
        var = _dot(hi, avg_ref[...]) + _dot(lo, avg_ref[...])
        y = d * lax.rsqrt(var + GN_EPS) * lnw_ref[:, sl] + lnb_ref[:, sl]
        y_ref[:, sl] = ((y + bv_ref[:, sl]) * g_ref[:, sl]).astype(y_ref.dtype)


def _rwkv_post(o_f, o_b, g, bv, lnw, lnb):
    npair, t, _ = o_f.shape
    dr = npair * LANES
    avg = _head_block_ones(1.0 / HEAD)
    pair_spec = pl.BlockSpec((npair, TM, LANES), lambda i: (0, i, 0))
    flat_spec = pl.BlockSpec((TM, dr), lambda i: (i, 0))
    vec = pl.BlockSpec((1, dr), lambda i: (0, 0))
    return pl.pallas_call(
        _rwkv_post_kernel,
        grid=(t // TM,),
        in_specs=[pair_spec, pair_spec, flat_spec, flat_spec, vec, vec,
                  pl.BlockSpec((LANES, LANES), lambda i: (0, 0))],
        out_specs=flat_spec,
        out_shape=jax.ShapeDtypeStruct((t, dr), BF16),
        compiler_params=_cparams(("arbitrary",)),
        name="rwkv_post",
    )(o_f, o_b, g, bv, lnw, lnb, avg)


def _mix_kernel(yf_ref, yr_ref, gf_ref, gr_ref, x_ref, mod_ref, gpost_ref, gpre_ref,
                wf_ref, wr_ref, wo_ref, wrh_ref, wrl_ref, xn_ref, h2_ref, sc_ref):
    a = _dot(yf_ref[...], wf_ref[...])
    b = _dot(yr_ref[...], wr_ref[...])
    merged = jax.nn.sigmoid(gf_ref[...]) * a + jax.nn.sigmoid(gr_ref[...]) * b
    m = _dot(merged.astype(BF16), wo_ref[...])
    mod = mod_ref[0]
    xn = x_ref[...] + mod[2:3] * _rms(m, gpost_ref[...])
    xn_ref[...] = xn
    h2 = _rms(xn, gpre_ref[...]) * (1.0 + mod[4:5]) + mod[3:4]
    h2_ref[...] = _pack_halves(h2)
    hi, lo = _split2(h2)
    logits = _dot(hi, wrh_ref[...]) + (_dot(hi, wrl_ref[...]) + _dot(lo, wrh_ref[...]))
    sc_ref[...] = jax.nn.sigmoid(logits)


def _mix(yf, yr, proj, gf_col, gr_col, x, mod, gpost, gpre, wf, wr, wo, wrh, wrl, n_ctx_rows, lat_len):
    t, d = x.shape
    tm = TM
    df = yf.shape[1]
    ne = wrh.shape[1]
    row = functools.partial(_seq_row, tile=tm, n_ctx_rows=n_ctx_rows, lat_len=lat_len)
    full = lambda a: pl.BlockSpec(a.shape, lambda i, _n=a.ndim: (0,) * _n, pipeline_mode=pl.Buffered(1))
    rowspec = lambda w, cb=0: pl.BlockSpec((tm, w), lambda i: (i, cb))
    return pl.pallas_call(
        _mix_kernel,
        grid=(t // tm,),
        in_specs=[rowspec(df), rowspec(df), rowspec(d, gf_col // d), rowspec(d, gr_col // d), rowspec(d),
                  pl.BlockSpec((1, 6, d), lambda i: (row(i), 0, 0)),
                  full(gpost), full(gpre), full(wf), full(wr), full(wo), full(wrh), full(wrl)],
        out_specs=[rowspec(d), rowspec(d // 2), rowspec(ne)],
        out_shape=[jax.ShapeDtypeStruct((t, d), F32), jax.ShapeDtypeStruct((t, d // 2), jnp.uint32),
                   jax.ShapeDtypeStruct((t, ne), F32)],
        compiler_params=_cparams(("arbitrary",)),
        name="mix_out",
    )(yf, yr, proj, proj, x, mod, gpost, gpre, wf, wr, wo, wrh, wrl)


def _route_kernel(sc_ref, bias_ref, idx_ref, wt_ref):
    sc = sc_ref[...]
    tm, ne = sc.shape
    per = ne // N_GROUPS
    lane_i = lax.broadcasted_iota(jnp.int32, (tm, ne), 1)
    lane = lane_i.astype(F32)
    gid = lane_i // per
    sel = sc + bias_ref[...]
    neg = -jnp.inf
    far = float(ne)

    def first_max(x):
        m = jnp.max(x, axis=-1, keepdims=True)
        i = jnp.min(jnp.where(x == m, lane, far), axis=-1, keepdims=True)
        return m, i

    gs = []
    for g in range(N_GROUPS):
        xg = jnp.where(gid == g, sel, neg)
        m1, i1 = first_max(xg)
        m2 = jnp.max(jnp.where(lane == i1, neg, xg), axis=-1, keepdims=True)
        gs.append(m1 + m2)
    keep = jnp.zeros((tm, ne), jnp.bool_)
    for g in range(N_GROUPS):
        rank = jnp.zeros((tm, 1), F32)
        for h in range(N_GROUPS):
            if h == g:
                continue
            ahead = (gs[h] >= gs[g]) if h < g else (gs[h] > gs[g])
            rank = rank + jnp.where(ahead, 1.0, 0.0)
        keep = jnp.logical_or(keep, jnp.logical_and(rank < TOPK_GROUPS, gid == g))
    cur = jnp.where(keep, sel, neg)
    lane_k = lax.broadcasted_iota(jnp.int32, (tm, TOP_K), 1)
    idx = jnp.zeros((tm, TOP_K), F32)
    wts = jnp.zeros((tm, TOP_K), F32)
    tot = jnp.zeros((tm, 1), F32)
    for k in range(TOP_K):
        _, ik = first_max(cur)
        hit = lane == ik
        wk = jnp.sum(jnp.where(hit, sc, 0.0), axis=-1, keepdims=True)
        cur = jnp.where(hit, neg, cur)
        idx = jnp.where(lane_k == k, ik, idx)
        wts = jnp.where(lane_k == k, wk, wts)
        tot = tot + wk
    idx_ref[...] = idx.astype(jnp.int32)
    wt_ref[...] = wts / tot * ROUTED_SCALE


def _route(scores, router_bias):
    t, ne = scores.shape
    spec = pl.BlockSpec((TM, TOP_K), lambda i: (i, 0))
    return pl.pallas_call(
        _route_kernel,
        grid=(t // TM,),
        in_specs=[pl.BlockSpec((TM, ne), lambda i: (i, 0)), pl.BlockSpec((1, ne), lambda i: (0, 0))],
        out_specs=[spec, spec],
        out_shape=[jax.ShapeDtypeStruct((t, TOP_K), jnp.int32), jax.ShapeDtypeStruct((t, TOP_K), F32)],
        compiler_params=_cparams(("arbitrary",)),
        name="route",
    )(scores, router_bias[None, :])


def _dispatch(eidx, ne):
    t, k = eidx.shape
    a = t * k
    nb = a // MOE_TM + ne
    eid = eidx.reshape(-1)
    counts = jnp.sum((eid[:, None] == jnp.arange(ne, dtype=jnp.int32)[None, :]).astype(jnp.int32), axis=0)
    padc = (-counts) % MOE_TM
    q = jnp.arange(MOE_TM, dtype=jnp.int32)
    pad_valid = q[None, :] < padc[:, None]
    pad_e = jnp.where(pad_valid, jnp.arange(ne, dtype=jnp.int32)[:, None], ne).reshape(-1)
    key_real = eid * (2 * a) + jnp.arange(a, dtype=jnp.int32)
    key_pad = pad_e * (2 * a) + a + jnp.arange(ne * MOE_TM, dtype=jnp.int32) % a
    keys = jnp.sort(jnp.concatenate([key_real, key_pad]))
    slot_e = keys // (2 * a)
    rem = keys % (2 * a)
    is_real = rem < a
    slot_a = jnp.where(is_real, rem, 0)
    slot_tok = slot_a // k
    blk_e = slot_e[::MOE_TM]
    blk_valid = (blk_e < ne).astype(jnp.int32)
    blk_e = jnp.minimum(blk_e, ne - 1).astype(jnp.int32)
    n_slots = nb * MOE_TM
    sort_key = jnp.where(is_real, rem, a)
    _, pos = lax.sort((sort_key, jnp.arange(n_slots, dtype=jnp.int32)), num_keys=1)
    pos = pos[:a].reshape(t, k)
    return slot_tok.astype(jnp.int32), blk_e, blk_valid, pos.astype(jnp.int32), nb


def _expert_kernel(be_ref, bv_ref, tok_ref, tokn_ref, h_hbm, w1_ref, w3_ref, w2_ref, y_ref,
                   xbuf, w1b, w3b, w2b, sem, *, nb):
    i = pl.program_id(0)
    slot = lax.rem(i, 2)

    def issue(tref, s):
        for rr in range(MOE_TM):
            pltpu.make_async_copy(h_hbm.at[pl.ds(tref[0, 0, rr], 1), :],
                                  xbuf.at[s, pl.ds(rr, 1), :], sem.at[s]).start(priority=rr % 2)

    @pl.when(jnp.logical_and(i == 0, bv_ref[0] == 1))
    def _():
        issue(tok_ref, 0)

    @pl.when(jnp.logical_and(i + 1 < nb, bv_ref[jnp.minimum(i + 1, nb - 1)] == 1))
    def _():
        issue(tokn_ref, 1 - slot)

    changed = jnp.logical_or(i == 0, be_ref[i] != be_ref[jnp.maximum(i - 1, 0)])

    @pl.when(jnp.logical_and(changed, bv_ref[i] == 1))
    def _():
        w1b[...] = w1_ref[0].astype(BF16)
        w3b[...] = w3_ref[0].astype(BF16)
        w2b[...] = w2_ref[0].astype(BF16)

    @pl.when(bv_ref[i] == 1)
    def _():
        for _ in range(MOE_TM):
            pltpu.make_async_copy(h_hbm.at[pl.ds(0, 1), :], xbuf.at[slot, pl.ds(0, 1), :],
                                  sem.at[slot]).wait()
        x = jnp.concatenate(_unpack_halves(xbuf[slot]), axis=1).astype(BF16)
        a = _dot(x, w1b[...])
        b = _dot(x, w3b[...])
        hb = (a * jax.nn.sigmoid(a) * b).astype(BF16)
        y_ref[...] = _pack_halves(_dot(hb, w2b[...]))

    @pl.when(bv_ref[i] == 0)
    def _():
        y_ref[...] = jnp.zeros_like(y_ref)


def _experts(h2p, slot_tok, blk_e, blk_valid, w1, w3, w2, nb):
    ne, d, de = w1.shape
    dp = d // 2
    tok3 = slot_tok.reshape(nb, 1, MOE_TM)
    smem_tok = lambda f: pl.BlockSpec((1, 1, MOE_TM), f, memory_space=pltpu.SMEM)
    grid_spec = pltpu.PrefetchScalarGridSpec(
        num_scalar_prefetch=2,
        grid=(nb,),
        in_specs=[smem_tok(lambda i, be, bv: (i, 0, 0)),
                  smem_tok(lambda i, be, bv: (jnp.minimum(i + 1, nb - 1), 0, 0)),
                  pl.BlockSpec(memory_space=pl.ANY),
                  pl.BlockSpec((1, d, de), lambda i, be, bv: (be[i], 0, 0)),
                  pl.BlockSpec((1, d, de), lambda i, be, bv: (be[i], 0, 0)),
                  pl.BlockSpec((1, de, d), lambda i, be, bv: (be[i], 0, 0))],
        out_specs=pl.BlockSpec((MOE_TM, dp), lambda i, be, bv: (i, 0)),
        scratch_shapes=[pltpu.VMEM((2, MOE_TM, dp), jnp.uint32),
                        pltpu.VMEM((d, de), BF16), pltpu.VMEM((d, de), BF16), pltpu.VMEM((de, d), BF16),
                        pltpu.SemaphoreType.DMA((2,))],
    )
    return pl.pallas_call(
        functools.partial(_expert_kernel, nb=nb),
        grid_spec=grid_spec,
        out_shape=jax.ShapeDtypeStruct((nb * MOE_TM, dp), jnp.uint32),
        compiler_params=_cparams(("arbitrary",)),
        name="experts",
    )(blk_e, blk_valid, tok3, tok3, h2p, w1, w3, w2)


def _combine_kernel(pos_ref, posn_ref, y_hbm, h2_ref, xn_ref, wt_ref, mod_ref, gpost_ref,
                    w1_ref, w3_ref, w2_ref, o_ref, ybuf, sem, *, n_tiles, k):
    i = pl.program_id(0)
    slot = lax.rem(i, 2)
    n_rows = CMB_TM * k

    def issue(pref, s):
        def batch(bi, carry):
            for u in range(DMA_UNROLL):
                rr = bi * DMA_UNROLL + u
                pltpu.make_async_copy(y_hbm.at[pl.ds(pref[0, 0, rr], 1), :],
                                      ybuf.at[s, pl.ds(rr, 1), :], sem.at[s]).start(priority=u % 2)
            return carry
        lax.fori_loop(0, n_rows // DMA_UNROLL, batch, 0)

    @pl.when(i == 0)
    def _():
        issue(pos_ref, 0)

    @pl.when(i + 1 < n_tiles)
    def _():
        issue(posn_ref, 1 - slot)

    x = jnp.concatenate(_unpack_halves(h2_ref[...]), axis=1).astype(BF16)
    a = _dot(x, w1_ref[...])
    b = _dot(x, w3_ref[...])
    f = _dot((a * jax.nn.sigmoid(a) * b).astype(BF16), w2_ref[...])

    for _ in range(n_rows):
        pltpu.make_async_copy(y_hbm.at[pl.ds(0, 1), :], ybuf.at[slot, pl.ds(0, 1), :], sem.at[slot]).wait()

    wt = wt_ref[...]
    f_lo = jnp.zeros((CMB_TM, x.shape[1] // 2), F32)
    f_hi = f_lo
    for kk in range(k):
        lo, hi = _unpack_halves(ybuf[slot, kk * CMB_TM:(kk + 1) * CMB_TM, :])
        f_lo = f_lo + wt[:, kk:kk + 1] * lo
        f_hi = f_hi + wt[:, kk:kk + 1] * hi
    f = f + jnp.concatenate([f_lo, f_hi], axis=1)
    mod = mod_ref[0]
    o_ref[...] = xn_ref[...] + mod[5:6] * _rms(f, gpost_ref[...])


def _combine(y, pos, wts, h2p, xn, mod, gpost, w1, w3, w2, n_ctx_rows, lat_len):
    t, d = xn.shape
    dp = d // 2
    k = pos.shape[1]
    n_tiles = t // CMB_TM
    pos3 = pos.reshape(n_tiles, CMB_TM, k).transpose(0, 2, 1).reshape(n_tiles, 1, CMB_TM * k)
    row = functools.partial(_seq_row, tile=CMB_TM, n_ctx_rows=n_ctx_rows, lat_len=lat_len)
    smem_pos = lambda f: pl.BlockSpec((1, 1, CMB_TM * k), f, memory_space=pltpu.SMEM)
    full = lambda a: pl.BlockSpec(a.shape, lambda i, _n=a.ndim: (0,) * _n)
    rowspec = lambda w: pl.BlockSpec((CMB_TM, w), lambda i: (i, 0))
    return pl.pallas_call(
        functools.partial(_combine_kernel, n_tiles=n_tiles, k=k),
        grid=(n_tiles,),
        in_specs=[smem_pos(lambda i: (i, 0, 0)),
                  smem_pos(lambda i: (jnp.minimum(i + 1, n_tiles - 1), 0, 0)),
                  pl.BlockSpec(memory_space=pl.ANY),
                  rowspec(dp), rowspec(d), rowspec(k),
                  pl.BlockSpec((1, 6, d), lambda i: (row(i), 0, 0)),
                  full(gpost), full(w1), full(w3), full(w2)],
        out_specs=rowspec(d),
        out_shape=jax.ShapeDtypeStruct((t, d), F32),
        scratch_shapes=[pltpu.VMEM((2, CMB_TM * k, dp), jnp.uint32), pltpu.SemaphoreType.DMA((2,))],
        compiler_params=_cparams(("arbitrary",)),
        name="combine",
    )(pos3, pos3, y, h2p, xn, wts, mod, gpost, w1, w3, w2)


def _pad_cols(w, n):
    return jnp.pad(w, ((0, 0), (0, n - w.shape[1])))


def _pad_rows(w, n):
    return jnp.pad(w, ((0, n - w.shape[0]), (0, 0)))


def _pair_state(s):
    n, nd, h, hv, hk = s.shape
    s = s.reshape(n, nd, h // 2, 2, hv, hk)
    z = jnp.zeros_like(s[:, :, :, 0])
    top = jnp.concatenate([s[:, :, :, 0], z], axis=-1)
    bot = jnp.concatenate([z, s[:, :, :, 1]], axis=-1)
    return jnp.concatenate([top, bot], axis=-2)


def _unpair_state(s):
    n, nd, hp = s.shape[:3]
    a = s[..., :HEAD, :HEAD]
    b = s[..., HEAD:, HEAD:]
    return jnp.stack([a, b], axis=3).reshape(n, nd, hp * 2, HEAD, HEAD)


def kernel(x_prompt, x_sample, state_rwkv, c, c_ctx, w_ada, b_ada, g_pre_mix, g_post_mix, g_pre_ffn, g_post_ffn, w_in, conv_rkv, decay_a, decay_b, decay_bias, iclr_a, iclr_b, iclr_bias, gate_a, gate_b, k_k, k_a, r_k, ln_x_w, ln_x_b, w_fourier_out, w_rwkv_out, w_out, w_router, router_bias, w1_exp, w3_exp, w2_exp, w1_sh, w3_sh, w2_sh):
    n_ctx, l_ctx, d = x_prompt.shape
    n_lat, l_lat, _ = x_sample.shape
    depth = w_ada.shape[0]
    assert l_ctx == TM and l_lat % TM == 0 and l_lat % GRID_W == 0
    n_ctx_rows = n_ctx * l_ctx
    t = n_ctx_rows + n_lat * l_lat
    dr = k_k.shape[1]
    df = w_fourier_out.shape[1]
    n_fg = 4
    gw = df // n_fg
    n_heads = dr // HEAD
    ne = w_router.shape[2]
    lora_r = decay_a.shape[3]
    assert lora_r <= LANES and gate_a.shape[2] == 2 * LANES

    x = jnp.concatenate([x_prompt.reshape(n_ctx_rows, d), x_sample.reshape(n_lat * l_lat, d)], axis=0)
    cond = jnp.concatenate([c_ctx[None, :], c, jnp.zeros((16 - 1 - n_lat, d), F32)], axis=0)
    states = []
    lat_state = state_rwkv
    for l in range(depth):
        mod = _ada(cond, w_ada[l].astype(BF16), b_ada[l][None, :]).reshape(16, 6, d)

        wi = w_in[l]
        lora_w = jnp.concatenate(
            [_pad_cols(decay_a[l, 0], LANES), _pad_cols(decay_a[l, 1], LANES),
             _pad_cols(iclr_a[l, 0], LANES), _pad_cols(iclr_a[l, 1], LANES), gate_a[l]], axis=1)
        lora_w = _pad_cols(lora_w, 1024)
        w_cat = jnp.concatenate([wi[:, df:df + 3 * dr], wi[:, :df], wi[:, df + 3 * dr:], lora_w], axis=1).astype(BF16)
        col_f = 3 * dr
        col_gf = col_f + df
        col_gr = col_gf + d
        col_lora = col_gr + d

        h = _prenorm(x, g_pre_mix[l][None, :], mod, n_ctx_rows, l_lat)
        proj = _matmul(h, w_cat, 512, 1024, F32, "in_proj")

        yf_ctx = _fourier_ctx(proj, col_f, n_ctx, l_ctx, gw, n_fg)
        yf_lat = _fourier_lat(proj, col_f, n_ctx_rows, n_lat, l_lat, gw, n_fg)
        yf = jnp.concatenate([yf_ctx, yf_lat], axis=0)

        db = jnp.stack([_pad_rows(decay_b[l, z], LANES) for z in range(2)]).astype(BF16)
        ib = jnp.stack([_pad_rows(iclr_b[l, z], LANES) for z in range(2)]).astype(BF16)
        prep = _rwkv_prep(proj, col_lora, t, dr, n_ctx_rows, l_lat, conv_rkv[l], db, ib,
                          gate_b[l].astype(BF16), decay_bias[l], iclr_bias[l], k_k[l][None, :],
                          k_a[l][None, :], r_k[l].reshape(1, dr))
        scan_in, gate, bv = prep[:11], prep[11], prep[12]
        s0 = jnp.concatenate([jnp.zeros((n_ctx, 2, n_heads // 2, LANES, LANES), F32),
                              _pair_state(lat_state[:, l])], axis=0)
        per = l_lat // TM
        seq_tiles = [(b, 1) for b in range(n_ctx)] + [(n_ctx + b * per, per) for b in range(n_lat)]
        o_f, o_b, s_fin = _rwkv_scan(scan_in, s0, seq_tiles)
        states.append(_unpair_state(s_fin[:n_ctx]))
        yr = _rwkv_post(o_f, o_b, gate, bv, ln_x_w[l][None, :], ln_x_b[l][None, :])

        wr_hi = w_router[l].astype(BF16)
        wr_lo = (w_router[l] - wr_hi.astype(F32)).astype(BF16)
        xn, h2, scores = _mix(yf, yr, proj, col_gf, col_gr, x, mod, g_post_mix[l][None, :],
                              g_pre_ffn[l][None, :], w_fourier_out[l].astype(BF16),
                              w_rwkv_out[l].astype(BF16), w_out[l].astype(BF16), wr_hi, wr_lo,
                              n_ctx_rows, l_lat)

        eidx, wts = _route(scores, router_bias[l])
        slot_tok, blk_e, blk_valid, pos, nb = _dispatch(eidx, ne)
        y = _experts(h2, slot_tok, blk_e, blk_valid, w1_exp[l], w3_exp[l], w2_exp[l], nb)
        x = _combine(y, pos, wts, h2, xn, mod, g_post_ffn[l][None, :], w1_sh[l].astype(BF16),
                     w3_sh[l].astype(BF16), w2_sh[l].astype(BF16), n_ctx_rows, l_lat)

    y_prompt = x[:n_ctx_rows].reshape(n_ctx, l_ctx, d)
    y_sample = x[n_ctx_rows:].reshape(n_lat, l_lat, d)
    return y_prompt, y_sample, jnp.stack(states, axis=1)
```

```python
import functools

import numpy as np
import jax
import jax.numpy as jnp
from jax import lax
from jax.experimental import pallas as pl
from jax.experimental.pallas import tpu as pltpu

F32 = jnp.float32
BF16 = jnp.bfloat16

HEAD = 64
LANES = 128
TM = 256
CHUNK = 64
SCAN_PAIRS = 4
DECAY_SCALE = 0.606531
GN_EPS = 64e-5
RMS_EPS = 1e-6
N_GROUPS = 8
TOPK_GROUPS = 4
TOP_K = 8
ROUTED_SCALE = 2.5
GRID_W = 64
MOE_TM = 256
CMB_TM = 128
VMEM_LIMIT = 56 * 1024 * 1024


def _cparams(sem):
    return pltpu.CompilerParams(dimension_semantics=sem, vmem_limit_bytes=VMEM_LIMIT)


def _dot(a, b):
    return jnp.dot(a, b, preferred_element_type=F32)


def _dot_nt(a, b):
    return lax.dot_general(a, b, (((1,), (1,)), ((), ())), preferred_element_type=F32)


def _split2(x):
    hi = x.astype(BF16)
    lo = (x - hi.astype(F32)).astype(BF16)
    return hi, lo


def _rms(x, g):
    return x * lax.rsqrt(jnp.mean(x * x, axis=-1, keepdims=True) + RMS_EPS) * g


def _pack_halves(x):
    n = x.shape[1] // 2
    bits = lax.bitcast_convert_type(x.astype(BF16).astype(F32), jnp.uint32)
    return (bits[:, n:] & jnp.uint32(0xFFFF0000)) | (bits[:, :n] >> 16)


def _unpack_halves(p):
    lo = lax.bitcast_convert_type(p << 16, F32)
    hi = lax.bitcast_convert_type(p & jnp.uint32(0xFFFF0000), F32)
    return lo, hi


def _seq_row(i, tile, n_ctx_rows, lat_len):
    t_ctx = n_ctx_rows // tile
    per = lat_len // tile
    return jnp.where(i < t_ctx, 0, 1 + (i - t_ctx) // per)


def _ada_kernel(c_ref, w_ref, b_ref, o_ref):
    c = c_ref[...]
    s = c * jax.nn.sigmoid(c)
    o_ref[...] = _dot(s.astype(BF16), w_ref[...]) + b_ref[...]


def _ada(cond, w, b):
    m, d = cond.shape
    n = w.shape[1]
    tn = 2048
    return pl.pallas_call(
        _ada_kernel,
        grid=(n // tn,),
        in_specs=[pl.BlockSpec((m, d), lambda j: (0, 0)),
                  pl.BlockSpec((d, tn), lambda j: (0, j)),
                  pl.BlockSpec((1, tn), lambda j: (0, j))],
        out_specs=pl.BlockSpec((m, tn), lambda j: (0, j)),
        out_shape=jax.ShapeDtypeStruct((m, n), F32),
        compiler_params=_cparams(("arbitrary",)),
        name="ada",
    )(cond, w, b)


def _prenorm_kernel(x_ref, g_ref, mod_ref, h_ref):
    m = mod_ref[0]
    y = _rms(x_ref[...], g_ref[...])
    h_ref[...] = (y * (1.0 + m[1:2]) + m[0:1]).astype(h_ref.dtype)


def _prenorm(x, g, mod, n_ctx_rows, lat_len):
    t, d = x.shape
    row = functools.partial(_seq_row, tile=TM, n_ctx_rows=n_ctx_rows, lat_len=lat_len)
    return pl.pallas_call(
        _prenorm_kernel,
        grid=(t // TM,),
        in_specs=[pl.BlockSpec((TM, d), lambda i: (i, 0)),
                  pl.BlockSpec((1, d), lambda i: (0, 0)),
                  pl.BlockSpec((1, 6, d), lambda i: (row(i), 0, 0))],
        out_specs=pl.BlockSpec((TM, d), lambda i: (i, 0)),
        out_shape=jax.ShapeDtypeStruct((t, d), BF16),
        compiler_params=_cparams(("arbitrary",)),
        name="prenorm",
    )(x, g, mod)


def _mm_kernel(x_ref, w_ref, o_ref):
    o_ref[...] = _dot(x_ref[...], w_ref[...]).astype(o_ref.dtype)


def _matmul(x, w, tm, tn, out_dtype, name):
    m, k = x.shape
    n = w.shape[1]
    return pl.pallas_call(
        _mm_kernel,
        grid=(n // tn, m // tm),
        in_specs=[pl.BlockSpec((tm, k), lambda j, i: (i, 0)),
                  pl.BlockSpec((k, tn), lambda j, i: (0, j))],
        out_specs=pl.BlockSpec((tm, tn), lambda j, i: (i, j)),
        out_shape=jax.ShapeDtypeStruct((m, n), out_dtype),
        compiler_params=_cparams(("arbitrary", "arbitrary")),
        name=name,
    )(x, w)


def _dft_mats(n):
    k = np.arange(n)
    ang = 2.0 * np.pi * ((k[:, None] * k[None, :]) % n) / n
    s = 1.0 / np.sqrt(n)
    return np.cos(ang) * s, np.sin(ang) * s


def _fourier_ctx_kernel(z_ref, cs_ref, ls_ref, y_ref):
    gw = z_ref.shape[1]
    ab = _dot(z_ref[...].astype(BF16), cs_ref[...])
    st = jnp.concatenate([ab[:, :gw], ab[:, gw:]], axis=0).astype(BF16)
    y_ref[...] = _dot(ls_ref[...], st).astype(y_ref.dtype)


def _fourier_ctx(proj, col0, n_seq, seq_len, gw, n_groups):
    cc, sc = _dft_mats(gw)
    cl, sl = _dft_mats(seq_len)
    cs = jnp.asarray(np.concatenate([cc, -sc], axis=1), BF16)
    ls = jnp.asarray(np.concatenate([cl, sl], axis=1), BF16)
    cb = col0 // gw
    return pl.pallas_call(
        _fourier_ctx_kernel,
        grid=(n_seq, n_groups),
        in_specs=[pl.BlockSpec((seq_len, gw), lambda b, g: (b, cb + g)),
                  pl.BlockSpec((gw, 2 * gw), lambda b, g: (0, 0)),
                  pl.BlockSpec((seq_len, 2 * seq_len), lambda b, g: (0, 0))],
        out_specs=pl.BlockSpec((seq_len, gw), lambda b, g: (b, g)),
        out_shape=jax.ShapeDtypeStruct((n_seq * seq_len, n_groups * gw), BF16),
        compiler_params=_cparams(("arbitrary", "arbitrary")),
        name="fourier_ctx",
    )(proj, cs, ls)


def _fourier_lat_a_kernel(z_ref, cs_ref, qs_ref, o_ref):
    gw = z_ref.shape[1]
    ab = _dot(z_ref[...].astype(BF16), cs_ref[...])
    for rr in range(TM // GRID_W):
        slab = ab[rr * GRID_W:(rr + 1) * GRID_W]
        swap = jnp.concatenate([slab[:, gw:], -slab[:, :gw]], axis=1)
        st = jnp.concatenate([slab, swap], axis=0).astype(BF16)
        out = _dot(qs_ref[...], st)
        o_ref[0, rr * GRID_W:(rr + 1) * GRID_W, :] = out[:, :gw].astype(o_ref.dtype)
        o_ref[1, rr * GRID_W:(rr + 1) * GRID_W, :] = out[:, gw:].astype(o_ref.dtype)


def _fourier_lat_b_kernel(x_ref, rs_ref, y_ref):
    st = jnp.concatenate([x_ref[0, 0], x_ref[1, 0]], axis=0)
    y_ref[0] = _dot(rs_ref[...], st).astype(y_ref.dtype)


def _fourier_lat(proj, col0, row0, n_seq, seq_len, gw, n_groups):
    rows = seq_len // GRID_W
    cc, sc = _dft_mats(gw)
    cq, sq = _dft_mats(GRID_W)
    cr, sr = _dft_mats(rows)
    cs = jnp.asarray(np.concatenate([cc, -sc], axis=1), BF16)
    qs = jnp.asarray(np.concatenate([cq, sq], axis=1), BF16)
    rs = jnp.asarray(np.concatenate([cr, sr], axis=1), BF16)
    cb = col0 // gw
    rb = row0 // TM
    per = seq_len // TM
    width = n_groups * gw
    ab = pl.pallas_call(
        _fourier_lat_a_kernel,
        grid=(n_seq * per, n_groups),
        in_specs=[pl.BlockSpec((TM, gw), lambda i, g: (rb + i, cb + g)),
                  pl.BlockSpec((gw, 2 * gw), lambda i, g: (0, 0)),
                  pl.BlockSpec((GRID_W, 2 * GRID_W), lambda i, g: (0, 0))],
        out_specs=pl.BlockSpec((2, TM, gw), lambda i, g: (0, i, g)),
        out_shape=jax.ShapeDtypeStruct((2, n_seq * seq_len, width), BF16),
        compiler_params=_cparams(("arbitrary", "arbitrary")),
        name="fourier_lat_cols",
    )(proj, cs, qs)
    flat = GRID_W * width
    tn = min(4096, flat)
    y = pl.pallas_call(
        _fourier_lat_b_kernel,
        grid=(n_seq, flat // tn),
        in_specs=[pl.BlockSpec((2, 1, rows, tn), lambda b, j: (0, b, 0, j)),
                  pl.BlockSpec((rows, 2 * rows), lambda b, j: (0, 0))],
        out_specs=pl.BlockSpec((1, rows, tn), lambda b, j: (b, 0, j)),
        out_shape=jax.ShapeDtypeStruct((n_seq, rows, flat), BF16),
        compiler_params=_cparams(("arbitrary", "arbitrary")),
        name="fourier_lat_rows",
    )(ab.reshape(2, n_seq, rows, flat), rs)
    return y.reshape(n_seq * seq_len, width)


def _group_sum(x, ones_ref):
    hi, lo = _split2(x)
    return _dot(hi, ones_ref[...]) + _dot(lo, ones_ref[...])


def _rwkv_prep_kernel(u_ref, up_ref, un_ref, lora_ref, conv_ref, db_ref, ib_ref, gb_ref,
                      dbias_ref, ibias_ref, kkw_ref, ka_ref, rk_ref, ones_ref, tri_ref,
                      at0_o, rt0_o, bt0_o, kt0_o, pt0_o, at1_o, rt1_o, bt1_o, kt1_o, pt1_o, v_o, g_o, bv_o,
                      *, t_ctx, per):
    i = pl.program_id(0)
    dr = v_o.shape[0] * LANES
    j = i - t_ctx
    is_first = jnp.logical_or(i < t_ctx, lax.rem(j, per) == 0)
    is_last = jnp.logical_or(i < t_ctx, lax.rem(j, per) == per - 1)
    u = u_ref[...]
    rows = lax.broadcasted_iota(jnp.int32, (TM, 1), 0)
    prev_row = jnp.where(is_first, 0.0, up_ref[7:8, :])
    next_row = jnp.where(is_last, 0.0, un_ref[0:1, :])
    u_prev = jnp.where(rows == 0, prev_row, pltpu.roll(u, 1, 0))
    u_next = jnp.where(rows == TM - 1, next_row, pltpu.roll(u, TM - 1, 0))
    cw = conv_ref[...]
    rkv = u_prev * cw[0:1] + u * cw[1:2] + u_next * cw[2:3]
    r = rkv[:, :dr]
    k = rkv[:, dr:2 * dr]
    v = rkv[:, 2 * dr:]

    lora = lora_ref[...]
    dbias = dbias_ref[...]
    ibias = ibias_ref[...]
    lws, avs = [], []
    for z in range(2):
        dz = jnp.tanh(lora[:, z * LANES:(z + 1) * LANES]).astype(BF16)
        lws.append(-DECAY_SCALE * jax.nn.sigmoid(dbias[z:z + 1] + _dot(dz, db_ref[z])))
        iz = lora[:, (2 + z) * LANES:(3 + z) * LANES].astype(BF16)
        avs.append(jax.nn.sigmoid(ibias[z:z + 1] + _dot(iz, ib_ref[z])))
    gate = _dot(jax.nn.sigmoid(lora[:, 4 * LANES:6 * LANES]).astype(BF16), gb_ref[...])

    kkr = k * kkw_ref[...]
    ka = ka_ref[...]
    k0 = k * (1.0 + (avs[0] - 1.0) * ka)
    k1 = k * (1.0 + (avs[1] - 1.0) * ka)
    bon_in = r * (k0 + k1) * rk_ref[...]
    g_o[...] = gate
    kds = (k0, k1)
    outs = ((at0_o, rt0_o, bt0_o, kt0_o, pt0_o), (at1_o, rt1_o, bt1_o, kt1_o, pt1_o))
    n_chunks = TM // CHUNK
    for p in range(dr // LANES):
        sl = slice(p * LANES, (p + 1) * LANES)
        kk_p = kkr[:, sl]
        ss = _group_sum(kk_p * kk_p, ones_ref)
        kk_p = kk_p / jnp.maximum(jnp.sqrt(ss), 1e-12)
        r_p = r[:, sl]
        v_o[p] = v[:, sl].astype(BF16)
        bv_o[:, sl] = _group_sum(bon_in[:, sl], ones_ref) * v[:, sl]
        for z in range(2):
            at_o, rt_o, bt_o, kt_o, pt_o = outs[z]
            lw = lws[z][:, sl]
            h0 = lw.astype(BF16)
            r1 = lw - h0.astype(F32)
            h1 = r1.astype(BF16)
            h2 = (r1 - h1.astype(F32)).astype(BF16)
            cum = _dot(tri_ref[z], h0) + _dot(tri_ref[z], h1) + _dot(tri_ref[z], h2)
            p_in = jnp.exp(cum)
            p_inv = jnp.exp(-cum)
            at_o[p] = (-kk_p * jnp.exp(cum - lw)).astype(BF16)
            rt_o[p] = (r_p * p_in).astype(BF16)
            bt_o[p] = (kk_p * avs[z][:, sl] * p_inv).astype(BF16)
            kt_o[p] = (kds[z][:, sl] * p_inv).astype(BF16)
            for ch in range(n_chunks):
                row = ch * CHUNK + (0 if z == 1 else CHUNK - 1)
                pt_o[p, ch * 8:(ch + 1) * 8, :] = jnp.broadcast_to(p_in[row:row + 1, :], (8, LANES))


def _head_block_ones(scale):
    m = np.zeros((LANES, LANES), np.float32)
    m[:HEAD, :HEAD] = scale
    m[HEAD:, HEAD:] = scale
    return jnp.asarray(m, BF16)


def _rwkv_prep(proj, lora_col, t, dr, n_ctx_rows, lat_len, conv, db, ib, gb, dbias, ibias, kkw, ka, rk):
    t_ctx = n_ctx_rows // TM
    per = lat_len // TM
    n_tiles = t // TM
    npair = dr // LANES
    w3 = 3 * dr
    lb = lora_col // 1024
    nb8 = t // 8
    pair = jax.ShapeDtypeStruct((npair, t, LANES), BF16)
    ptot = jax.ShapeDtypeStruct((npair, t // 8, LANES), F32)
    flat = jax.ShapeDtypeStruct((t, dr), F32)
    full = lambda a: pl.BlockSpec(a.shape, lambda i, _n=a.ndim: (0,) * _n)
    ones = _head_block_ones(1.0)
    idx = np.arange(TM)
    same = (idx[:, None] // CHUNK) == (idx[None, :] // CHUNK)
    tri = jnp.asarray(np.stack([same & (idx[None, :] <= idx[:, None]),
                                same & (idx[None, :] >= idx[:, None])]).astype(np.float32), BF16)
    pair_spec = pl.BlockSpec((npair, TM, LANES), lambda i: (0, i, 0))
    ptot_spec = pl.BlockSpec((npair, TM // 8, LANES), lambda i: (0, i, 0))
    flat_spec = pl.BlockSpec((TM, dr), lambda i: (i, 0))
    dir_specs = [pair_spec] * 4 + [ptot_spec]
    dir_shapes = [pair] * 4 + [ptot]
    return pl.pallas_call(
        functools.partial(_rwkv_prep_kernel, t_ctx=t_ctx, per=per),
        grid=(n_tiles,),
        in_specs=[pl.BlockSpec((TM, w3), lambda i: (i, 0)),
                  pl.BlockSpec((8, w3), lambda i: (jnp.maximum(i * (TM // 8) - 1, 0), 0)),
                  pl.BlockSpec((8, w3), lambda i: (jnp.minimum((i + 1) * (TM // 8), nb8 - 1), 0)),
                  pl.BlockSpec((TM, 1024), lambda i: (i, lb)),
                  full(conv), full(db), full(ib), full(gb), full(dbias), full(ibias),
                  full(kkw), full(ka), full(rk), full(ones), full(tri)],
        out_specs=dir_specs * 2 + [pair_spec, flat_spec, flat_spec],
        out_shape=dir_shapes * 2 + [pair, flat, flat],
        compiler_params=_cparams(("arbitrary",)),
        name="rwkv_prep",
    )(proj, proj, proj, proj, conv, db, ib, gb, dbias, ibias, kkw, ka, rk, ones, tri)


def _scan_chunks(chains, m0):
    c = chains[0][0].shape[0]
    n2 = 2 * c
    zero = jnp.zeros_like(chains[0][0])
    rng = range(len(chains))

    def two_heads(x):
        return jnp.concatenate([jnp.where(m0, x, zero), jnp.where(m0, zero, x)], axis=0)

    vb16 = [two_heads(ch[4]) for ch in chains]
    lhs = [jnp.concatenate([two_heads(ch[0]), two_heads(ch[1])], axis=0) for ch in chains]
    rhs = [jnp.concatenate([two_heads(ch[2]), two_heads(ch[3])], axis=0) for ch in chains]
    gram = [_dot_nt(lhs[i], rhs[i]) for i in rng]
    a_s = [_dot_nt(lhs[i], chains[i][6].astype(BF16)) for i in rng]
    l_ka = [jnp.where(chains[i][7], gram[i][:n2, n2:], 0.0).astype(BF16) for i in rng]
    lp = [jnp.where(chains[i][7], gram[i][:n2, :n2], 0.0).astype(BF16) for i in rng]
    m_rk = [jnp.concatenate([jnp.where(chains[i][8], gram[i][n2:, :n2], 0.0),
                             jnp.where(chains[i][8], gram[i][n2:, n2:], 0.0)], axis=1).astype(BF16)
            for i in rng]
    x = [a_s[i][:n2] + _dot(l_ka[i], vb16[i]) for i in rng]
    n_iter = int(np.log2(c))
    for it in range(n_iter):
        if it < n_iter - 1:
            both = [_dot(lp[i], jnp.concatenate([lp[i], x[i].astype(BF16)], axis=1)) for i in rng]
            lp = [both[i][:, :n2].astype(BF16) for i in rng]
            x = [x[i] + both[i][:, n2:] for i in rng]
        else:
            x = [x[i] + _dot(lp[i], x[i].astype(BF16)) for i in rng]
    o_bd = [a_s[i][n2:] + _dot(m_rk[i], jnp.concatenate([x[i].astype(BF16), vb16[i]], axis=0)) for i in rng]
    uvt = [jnp.concatenate([x[i].T, vb16[i].astype(F32).T], axis=1).astype(BF16) for i in rng]
    s_new = [(chains[i][6] + _dot(uvt[i], rhs[i])) * chains[i][5] for i in rng]
    return [o_bd[i][:c] + o_bd[i][c:] for i in rng], s_new


def _rwkv_scan_kernel(ft_ref, bt_ref, sq_ref, first_ref, last_ref,
                      atf, rtf, btf, ktf, ptf, vf, atb, rtb, btb, ktb, ptb, vb, s0_ref,
                      of_ref, ob_ref, sout_ref, s_scr):
    it = pl.program_id(1)
    npp = s_scr.shape[1]

    @pl.when(first_ref[it] == 1)
    def _():
        s_scr[...] = s0_ref[0]

    c = CHUNK
    n2 = 2 * c
    ri = lax.broadcasted_iota(jnp.int32, (n2, n2), 0)
    ci = lax.broadcasted_iota(jnp.int32, (n2, n2), 1)
    same = (ri // c) == (ci // c)
    rp = ri % c
    cp = ci % c
    m0 = lax.broadcasted_iota(jnp.int32, (c, LANES), 1) < HEAD
    n_chunks = TM // c
    masks = ((jnp.logical_and(same, cp < rp), jnp.logical_and(same, cp <= rp)),
             (jnp.logical_and(same, cp > rp), jnp.logical_and(same, cp >= rp)))
    dirs = ((atf, rtf, btf, ktf, ptf, vf, of_ref), (atb, rtb, btb, ktb, ptb, vb, ob_ref))

    def body(ch, carry):
        chains, dests = [], []
        for q in range(npp):
            for z in range(2):
                a_r, r_r, b_r, k_r, p_r, v_r, o_r = dirs[z]
                cc = ch if z == 0 else n_chunks - 1 - ch
                sl = pl.ds(pl.multiple_of(cc * c, c), c)
                p_tot = p_r[q, pl.ds(pl.multiple_of(cc * 8, 8), 1), :]
                chains.append((a_r[q, sl, :], r_r[q, sl, :], b_r[q, sl, :], k_r[q, sl, :], v_r[q, sl, :],
                               p_tot, s_scr[z, q], masks[z][0], masks[z][1]))
                dests.append((o_r, q, sl, z))
        outs, states = _scan_chunks(chains, m0)
        for (o_r, q, sl, z), o, s_new in zip(dests, outs, states):
            o_r[q, sl, :] = o
            s_scr[z, q] = s_new
        return carry

    lax.fori_loop(0, n_chunks, body, 0)

    @pl.when(last_ref[it] == 1)
    def _():
        sout_ref[0] = s_scr[...]


def _rwkv_scan(prep, s0, seq_tiles):
    at0, rt0, bt0, kt0, pt0, at1, rt1, bt1, kt1, pt1, v = prep
    npair, t, _ = v.shape
    npp = SCAN_PAIRS
    ft, bt, sq, first, last = [], [], [], [], []
    for si, (t0, n) in enumerate(seq_tiles):
        for j in range(n):
            ft.append(t0 + j)
            bt.append(t0 + n - 1 - j)
            sq.append(si)
            first.append(int(j == 0))
            last.append(int(j == n - 1))
    tabs = [jnp.asarray(np.array(a, np.int32)) for a in (ft, bt, sq, first, last)]
    fwd = pl.BlockSpec((npp, TM, LANES), lambda p, i, ft, bt, sq, fi, la: (p, ft[i], 0))
    bwd = pl.BlockSpec((npp, TM, LANES), lambda p, i, ft, bt, sq, fi, la: (p, bt[i], 0))
    fwd_p = pl.BlockSpec((npp, TM // 8, LANES), lambda p, i, ft, bt, sq, fi, la: (p, ft[i], 0))
    bwd_p = pl.BlockSpec((npp, TM // 8, LANES), lambda p, i, ft, bt, sq, fi, la: (p, bt[i], 0))
    st = pl.BlockSpec((1, 2, npp, LANES, LANES), lambda p, i, ft, bt, sq, fi, la: (sq[i], 0, p, 0, 0))
    o_shape = jax.ShapeDtypeStruct((npair, t, LANES), F32)
    grid_spec = pltpu.PrefetchScalarGridSpec(
        num_scalar_prefetch=5,
        grid=(npair // npp, len(ft)),
        in_specs=[fwd] * 4 + [fwd_p, fwd] + [bwd] * 4 + [bwd_p, bwd] + [st],
        out_specs=[fwd, bwd, st],
        scratch_shapes=[pltpu.VMEM((2, npp, LANES, LANES), F32)],
    )
    return pl.pallas_call(
        _rwkv_scan_kernel,
        grid_spec=grid_spec,
        out_shape=[o_shape, o_shape, jax.ShapeDtypeStruct(s0.shape, F32)],
        compiler_params=_cparams(("arbitrary", "arbitrary")),
        name="rwkv_scan",
    )(*tabs, at0, rt0, bt0, kt0, pt0, v, at1, rt1, bt1, kt1, pt1, v, s0)


def _rwkv_post_kernel(of_ref, ob_ref, g_ref, bv_ref, lnw_ref, lnb_ref, avg_ref, y_ref):
    for p in range(of_ref.shape[0]):
        sl = slice(p * LANES, (p + 1) * LANES)
        o = of_ref[p] + ob_ref[p]
        hi, lo = _split2(o)
        mu = _dot(hi, avg_ref[...]) + _dot(lo, avg_ref[...])
        d = o - mu
        hi, lo = _split2(d * d)
        var = _dot(hi, avg_ref[...]) + _dot(lo, avg_ref[...])
        y = d * lax.rsqrt(var + GN_EPS) * lnw_ref[:, sl] + lnb_ref[:, sl]
        y_ref[:, sl] = ((y + bv_ref[:, sl]) * g_ref[:, sl]).astype(y_ref.dtype)


def _rwkv_post(o_f, o_b, g, bv, lnw, lnb):
    npair, t, _ = o_f.shape
    dr = npair * LANES
    avg = _head_block_ones(1.0 / HEAD)
    pair_spec = pl.BlockSpec((npair, TM, LANES), lambda i: (0, i, 0))
    flat_spec = pl.BlockSpec((TM, dr), lambda i: (i, 0))
    vec = pl.BlockSpec((1, dr), lambda i: (0, 0))
    return pl.pallas_call(
        _rwkv_post_kernel,
        grid=(t // TM,),
        in_specs=[pair_spec, pair_spec, flat_spec, flat_spec, vec, vec,
                  pl.BlockSpec((LANES, LANES), lambda i: (0, 0))],
        out_specs=flat_spec,
        out_shape=jax.ShapeDtypeStruct((t, dr), BF16),
        compiler_params=_cparams(("arbitrary",)),
        name="rwkv_post",
    )(o_f, o_b, g, bv, lnw, lnb, avg)


def _mix_kernel(yf_ref, yr_ref, gf_ref, gr_ref, x_ref, mod_ref, gpost_ref, gpre_ref,
                wf_ref, wr_ref, wo_ref, wrh_ref, wrl_ref, xn_ref, h2_ref, sc_ref):
    a = _dot(yf_ref[...], wf_ref[...])
    b = _dot(yr_ref[...], wr_ref[...])
    merged = jax.nn.sigmoid(gf_ref[...]) * a + jax.nn.sigmoid(gr_ref[...]) * b
    m = _dot(merged.astype(BF16), wo_ref[...])
    mod = mod_ref[0]
    xn = x_ref[...] + mod[2:3] * _rms(m, gpost_ref[...])
    xn_ref[...] = xn
    h2 = _rms(xn, gpre_ref[...]) * (1.0 + mod[4:5]) + mod[3:4]
    h2_ref[...] = _pack_halves(h2)
    hi, lo = _split2(h2)
    logits = _dot(hi, wrh_ref[...]) + (_dot(hi, wrl_ref[...]) + _dot(lo, wrh_ref[...]))
    sc_ref[...] = jax.nn.sigmoid(logits)


def _mix(yf, yr, proj, gf_col, gr_col, x, mod, gpost, gpre, wf, wr, wo, wrh, wrl, n_ctx_rows, lat_len):
    t, d = x.shape
    tm = TM
    df = yf.shape[1]
    ne = wrh.shape[1]
    row = functools.partial(_seq_row, tile=tm, n_ctx_rows=n_ctx_rows, lat_len=lat_len)
    full = lambda a: pl.BlockSpec(a.shape, lambda i, _n=a.ndim: (0,) * _n, pipeline_mode=pl.Buffered(1))
    rowspec = lambda w, cb=0: pl.BlockSpec((tm, w), lambda i: (i, cb))
    return pl.pallas_call(
        _mix_kernel,
        grid=(t // tm,),
        in_specs=[rowspec(df), rowspec(df), rowspec(d, gf_col // d), rowspec(d, gr_col // d), rowspec(d),
                  pl.BlockSpec((1, 6, d), lambda i: (row(i), 0, 0)),
                  full(gpost), full(gpre), full(wf), full(wr), full(wo), full(wrh), full(wrl)],
        out_specs=[rowspec(d), rowspec(d // 2), rowspec(ne)],
        out_shape=[jax.ShapeDtypeStruct((t, d), F32), jax.ShapeDtypeStruct((t, d // 2), jnp.uint32),
                   jax.ShapeDtypeStruct((t, ne), F32)],
        compiler_params=_cparams(("arbitrary",)),
        name="mix_out",
    )(yf, yr, proj, proj, x, mod, gpost, gpre, wf, wr, wo, wrh, wrl)


def _route_kernel(sc_ref, bias_ref, idx_ref, wt_ref):
    sc = sc_ref[...]
    tm, ne = sc.shape
    per = ne // N_GROUPS
    lane_i = lax.broadcasted_iota(jnp.int32, (tm, ne), 1)
    lane = lane_i.astype(F32)
    gid = lane_i // per
    sel = sc + bias_ref[...]
    neg = -jnp.inf
    far = float(ne)

    def first_max(x):
        m = jnp.max(x, axis=-1, keepdims=True)
        i = jnp.min(jnp.where(x == m, lane, far), axis=-1, keepdims=True)
        return m, i

    gs = []
    for g in range(N_GROUPS):
        xg = jnp.where(gid == g, sel, neg)
        m1, i1 = first_max(xg)
        m2 = jnp.max(jnp.where(lane == i1, neg, xg), axis=-1, keepdims=True)
        gs.append(m1 + m2)
    keep = jnp.zeros((tm, ne), jnp.bool_)
    for g in range(N_GROUPS):
        rank = jnp.zeros((tm, 1), F32)
        for h in range(N_GROUPS):
            if h == g:
                continue
            ahead = (gs[h] >= gs[g]) if h < g else (gs[h] > gs[g])
            rank = rank + jnp.where(ahead, 1.0, 0.0)
        keep = jnp.logical_or(keep, jnp.logical_and(rank < TOPK_GROUPS, gid == g))
    cur = jnp.where(keep, sel, neg)
    lane_k = lax.broadcasted_iota(jnp.int32, (tm, TOP_K), 1)
    idx = jnp.zeros((tm, TOP_K), F32)
    wts = jnp.zeros((tm, TOP_K), F32)
    tot = jnp.zeros((tm, 1), F32)
    for k in range(TOP_K):
        _, ik = first_max(cur)
        hit = lane == ik
        wk = jnp.sum(jnp.where(hit, sc, 0.0), axis=-1, keepdims=True)
        cur = jnp.where(hit, neg, cur)
        idx = jnp.where(lane_k == k, ik, idx)
        wts = jnp.where(lane_k == k, wk, wts)
        tot = tot + wk
    idx_ref[...] = idx.astype(jnp.int32)
    wt_ref[...] = wts / tot * ROUTED_SCALE


def _route(scores, router_bias):
    t, ne = scores.shape
    spec = pl.BlockSpec((TM, TOP_K), lambda i: (i, 0))
    return pl.pallas_call(
        _route_kernel,
        grid=(t // TM,),
        in_specs=[pl.BlockSpec((TM, ne), lambda i: (i, 0)), pl.BlockSpec((1, ne), lambda i: (0, 0))],
        out_specs=[spec, spec],
        out_shape=[jax.ShapeDtypeStruct((t, TOP_K), jnp.int32), jax.ShapeDtypeStruct((t, TOP_K), F32)],
        compiler_params=_cparams(("arbitrary",)),
        name="route",
    )(scores, router_bias[None, :])


def _dispatch(eidx, ne):
    t, k = eidx.shape
    a = t * k
    nb = a // MOE_TM + ne
    eid = eidx.reshape(-1)
    counts = jnp.sum((eid[:, None] == jnp.arange(ne, dtype=jnp.int32)[None, :]).astype(jnp.int32), axis=0)
    padc = (-counts) % MOE_TM
    q = jnp.arange(MOE_TM, dtype=jnp.int32)
    pad_valid = q[None, :] < padc[:, None]
    pad_e = jnp.where(pad_valid, jnp.arange(ne, dtype=jnp.int32)[:, None], ne).reshape(-1)
    key_real = eid * (2 * a) + jnp.arange(a, dtype=jnp.int32)
    key_pad = pad_e * (2 * a) + a + jnp.arange(ne * MOE_TM, dtype=jnp.int32) % a
    keys = jnp.sort(jnp.concatenate([key_real, key_pad]))
    slot_e = keys // (2 * a)
    rem = keys % (2 * a)
    is_real = rem < a
    slot_a = jnp.where(is_real, rem, 0)
    slot_tok = slot_a // k
    blk_e = slot_e[::MOE_TM]
    blk_valid = (blk_e < ne).astype(jnp.int32)
    blk_e = jnp.minimum(blk_e, ne - 1).astype(jnp.int32)
    cand = jnp.where(counts > 0, jnp.arange(ne, dtype=jnp.int32), ne)
    suffix_min = lax.cummin(cand[::-1])[::-1]
    next_e = jnp.concatenate([suffix_min[1:], jnp.full((1,), ne, jnp.int32)])
    blk_next = next_e[blk_e]
    blk_next = jnp.where(blk_next < ne, blk_next, -1).astype(jnp.int32)
    n_slots = nb * MOE_TM
    sort_key = jnp.where(is_real, rem, a)
    _, pos = lax.sort((sort_key, jnp.arange(n_slots, dtype=jnp.int32)), num_keys=1)
    pos = pos[:a].reshape(t, k)
    return slot_tok.astype(jnp.int32), blk_e, blk_valid, blk_next, pos.astype(jnp.int32), nb


def _expert_kernel(be_ref, bv_ref, nx_ref, tok_ref, h_hbm, w1_hbm, w3_hbm, w2_hbm, y_ref,
                   xbuf0, xbuf1, wf1, wf3, wf2, w1b, w3b, w2b, gsem, wsem, *, nb):
    j = pl.program_id(0)
    jm = jnp.maximum(j - 1, 0)
    issue_ok = jnp.logical_and(j < nb, bv_ref[jnp.minimum(j, nb - 1)] == 1)
    comp_ok = jnp.logical_and(j >= 1, bv_ref[jm] == 1)
    par = lax.rem(j, 2)
    xbufs = (xbuf0, xbuf1)

    def weight_copies(e):
        return (pltpu.make_async_copy(w1_hbm.at[e], wf1, wsem.at[0]),
                pltpu.make_async_copy(w3_hbm.at[e], wf3, wsem.at[1]),
                pltpu.make_async_copy(w2_hbm.at[e], wf2, wsem.at[2]))

    @pl.when(jnp.logical_and(j == 0, bv_ref[0] == 1))
    def _():
        for cp in weight_copies(be_ref[0]):
            cp.start(priority=1)

    run_start = jnp.logical_and(comp_ok, jnp.logical_or(jm == 0, be_ref[jm] != be_ref[jnp.maximum(jm - 1, 0)]))

    @pl.when(run_start)
    def _():
        for cp in weight_copies(be_ref[jm]):
            cp.wait()
        w1b[...] = wf1[...].astype(BF16)
        w3b[...] = wf3[...].astype(BF16)
        w2b[...] = wf2[...].astype(BF16)

        @pl.when(nx_ref[jm] >= 0)
        def _():
            for cp in weight_copies(nx_ref[jm]):
                cp.start(priority=1)

    def gather_wait(p):
        for _ in range(MOE_TM):
            pltpu.make_async_copy(h_hbm.at[pl.ds(0, 1), :], xbufs[p].at[pl.ds(0, 1), :], gsem.at[p]).wait()

    def gather_start(p):
        for rr in range(MOE_TM):
            pltpu.make_async_copy(h_hbm.at[pl.ds(tok_ref[0, 0, rr], 1), :],
                                  xbufs[p].at[pl.ds(rr, 1), :], gsem.at[p]).start(priority=0)

    def compute(p):
        x = jnp.concatenate(_unpack_halves(xbufs[p][...]), axis=1).astype(BF16)
        a = _dot(x, w1b[...])
        b = _dot(x, w3b[...])
        hb = (a * jax.nn.sigmoid(a) * b).astype(BF16)
        y_ref[...] = _pack_halves(_dot(hb, w2b[...]))

    for p in range(2):
        @pl.when(jnp.logical_and(jnp.logical_and(issue_ok, comp_ok), par == p))
        def _(p=p):
            gather_wait(1 - p)
            gather_start(p)
            compute(1 - p)

        @pl.when(jnp.logical_and(jnp.logical_and(jnp.logical_not(issue_ok), comp_ok), par == p))
        def _(p=p):
            gather_wait(1 - p)
            compute(1 - p)

    @pl.when(jnp.logical_and(issue_ok, j == 0))
    def _():
        gather_start(0)

    @pl.when(jnp.logical_and(j >= 1, bv_ref[jm] == 0))
    def _():
        y_ref[...] = jnp.zeros_like(y_ref)


def _experts(h2p, slot_tok, blk_e, blk_valid, blk_next, w1, w3, w2, nb):
    ne, d, de = w1.shape
    dp = d // 2
    tok3 = slot_tok.reshape(nb, 1, MOE_TM)
    grid_spec = pltpu.PrefetchScalarGridSpec(
        num_scalar_prefetch=3,
        grid=(nb + 1,),
        in_specs=[pl.BlockSpec((1, 1, MOE_TM), lambda j, be, bv, nx: (jnp.minimum(j, nb - 1), 0, 0),
                               memory_space=pltpu.SMEM),
                  pl.BlockSpec(memory_space=pl.ANY), pl.BlockSpec(memory_space=pl.ANY),
                  pl.BlockSpec(memory_space=pl.ANY), pl.BlockSpec(memory_space=pl.ANY)],
        out_specs=pl.BlockSpec((MOE_TM, dp), lambda j, be, bv, nx: (jnp.maximum(j - 1, 0), 0)),
        scratch_shapes=[pltpu.VMEM((MOE_TM, dp), jnp.uint32), pltpu.VMEM((MOE_TM, dp), jnp.uint32),
                        pltpu.VMEM((d, de), F32), pltpu.VMEM((d, de), F32), pltpu.VMEM((de, d), F32),
                        pltpu.VMEM((d, de), BF16), pltpu.VMEM((d, de), BF16), pltpu.VMEM((de, d), BF16),
                        pltpu.SemaphoreType.DMA((2,)), pltpu.SemaphoreType.DMA((3,))],
    )
    return pl.pallas_call(
        functools.partial(_expert_kernel, nb=nb),
        grid_spec=grid_spec,
        out_shape=jax.ShapeDtypeStruct((nb * MOE_TM, dp), jnp.uint32),
        compiler_params=_cparams(("arbitrary",)),
        name="experts",
    )(blk_e, blk_valid, blk_next, tok3, h2p, w1, w3, w2)


def _combine_kernel(pos_ref, y_hbm, h2_ref, xn_ref, wt_ref, mod_ref, gpost_ref,
                    w1_ref, w3_ref, w2_ref, o_ref, ybuf0, ybuf1, sem, *, n_tiles, k):
    j = pl.program_id(0)
    par = lax.rem(j, 2)
    n_rows = CMB_TM * k
    ybufs = (ybuf0, ybuf1)

    def gather_wait(p):
        for _ in range(n_rows):
            pltpu.make_async_copy(y_hbm.at[pl.ds(0, 1), :], ybufs[p].at[pl.ds(0, 1), :], sem.at[p]).wait()

    def gather_start(p):
        for rr in range(n_rows):
            pltpu.make_async_copy(y_hbm.at[pl.ds(pos_ref[0, 0, rr], 1), :],
                                  ybufs[p].at[pl.ds(rr, 1), :], sem.at[p]).start(priority=rr % 2)

    def compute(p):
        x = jnp.concatenate(_unpack_halves(h2_ref[...]), axis=1).astype(BF16)
        a = _dot(x, w1_ref[...])
        b = _dot(x, w3_ref[...])
        f = _dot((a * jax.nn.sigmoid(a) * b).astype(BF16), w2_ref[...])
        wt = wt_ref[...]
        f_lo = jnp.zeros((CMB_TM, x.shape[1] // 2), F32)
        f_hi = f_lo
        for kk in range(k):
            lo, hi = _unpack_halves(ybufs[p][kk * CMB_TM:(kk + 1) * CMB_TM, :])
            f_lo = f_lo + wt[:, kk:kk + 1] * lo
            f_hi = f_hi + wt[:, kk:kk + 1] * hi
        f = f + jnp.concatenate([f_lo, f_hi], axis=1)
        mod = mod_ref[0]
        o_ref[...] = xn_ref[...] + mod[5:6] * _rms(f, gpost_ref[...])

    @pl.when(j == 0)
    def _():
        gather_start(0)

    for p in range(2):
        @pl.when(jnp.logical_and(jnp.logical_and(j >= 1, j < n_tiles), par == p))
        def _(p=p):
            gather_wait(1 - p)
            gather_start(p)
            compute(1 - p)

    @pl.when(j == n_tiles)
    def _():
        gather_wait((n_tiles - 1) % 2)
        compute((n_tiles - 1) % 2)


def _combine(y, pos, wts, h2p, xn, mod, gpost, w1, w3, w2, n_ctx_rows, lat_len):
    t, d = xn.shape
    dp = d // 2
    k = pos.shape[1]
    n_tiles = t // CMB_TM
    pos3 = pos.reshape(n_tiles, CMB_TM, k).transpose(0, 2, 1).reshape(n_tiles, 1, CMB_TM * k)
    row = functools.partial(_seq_row, tile=CMB_TM, n_ctx_rows=n_ctx_rows, lat_len=lat_len)
    prev = lambda j: jnp.maximum(j - 1, 0)
    full = lambda a: pl.BlockSpec(a.shape, lambda j, _n=a.ndim: (0,) * _n, pipeline_mode=pl.Buffered(1))
    rowspec = lambda w: pl.BlockSpec((CMB_TM, w), lambda j: (prev(j), 0))
    return pl.pallas_call(
        functools.partial(_combine_kernel, n_tiles=n_tiles, k=k),
        grid=(n_tiles + 1,),
        in_specs=[pl.BlockSpec((1, 1, CMB_TM * k), lambda j: (jnp.minimum(j, n_tiles - 1), 0, 0),
                               memory_space=pltpu.SMEM),
                  pl.BlockSpec(memory_space=pl.ANY),
                  rowspec(dp), rowspec(d), rowspec(k),
                  pl.BlockSpec((1, 6, d), lambda j: (row(prev(j)), 0, 0)),
                  full(gpost), full(w1), full(w3), full(w2)],
        out_specs=rowspec(d),
        out_shape=jax.ShapeDtypeStruct((t, d), F32),
        scratch_shapes=[pltpu.VMEM((CMB_TM * k, dp), jnp.uint32), pltpu.VMEM((CMB_TM * k, dp), jnp.uint32),
                        pltpu.SemaphoreType.DMA((2,))],
        compiler_params=_cparams(("arbitrary",)),
        name="combine",
    )(pos3, y, h2p, xn, wts, mod, gpost, w1, w3, w2)


def _pad_cols(w, n):
    return jnp.pad(w, ((0, 0), (0, n - w.shape[1])))


def _pad_rows(w, n):
    return jnp.pad(w, ((0, n - w.shape[0]), (0, 0)))


def _pair_state(s):
    n, nd, h, hv, hk = s.shape
    s = s.reshape(n, nd, h // 2, 2, hv, hk)
    z = jnp.zeros_like(s[:, :, :, 0])
    top = jnp.concatenate([s[:, :, :, 0], z], axis=-1)
    bot = jnp.concatenate([z, s[:, :, :, 1]], axis=-1)
    return jnp.concatenate([top, bot], axis=-2)


def _unpair_state(s):
    n, nd, hp = s.shape[:3]
    a = s[..., :HEAD, :HEAD]
    b = s[..., HEAD:, HEAD:]
    return jnp.stack([a, b], axis=3).reshape(n, nd, hp * 2, HEAD, HEAD)


def kernel(x_prompt, x_sample, state_rwkv, c, c_ctx, w_ada, b_ada, g_pre_mix, g_post_mix, g_pre_ffn, g_post_ffn, w_in, conv_rkv, decay_a, decay_b, decay_bias, iclr_a, iclr_b, iclr_bias, gate_a, gate_b, k_k, k_a, r_k, ln_x_w, ln_x_b, w_fourier_out, w_rwkv_out, w_out, w_router, router_bias, w1_exp, w3_exp, w2_exp, w1_sh, w3_sh, w2_sh):
    n_ctx, l_ctx, d = x_prompt.shape
    n_lat, l_lat, _ = x_sample.shape
    depth = w_ada.shape[0]
    assert l_ctx == TM and l_lat % TM == 0 and l_lat % GRID_W == 0
    n_ctx_rows = n_ctx * l_ctx
    t = n_ctx_rows + n_lat * l_lat
    dr = k_k.shape[1]
    df = w_fourier_out.shape[1]
    n_fg = 4
    gw = df // n_fg
    n_heads = dr // HEAD
    ne = w_router.shape[2]
    lora_r = decay_a.shape[3]
    assert lora_r <= LANES and gate_a.shape[2] == 2 * LANES

    x = jnp.concatenate([x_prompt.reshape(n_ctx_rows, d), x_sample.reshape(n_lat * l_lat, d)], axis=0)
    cond = jnp.concatenate([c_ctx[None, :], c, jnp.zeros((16 - 1 - n_lat, d), F32)], axis=0)
    states = []
    lat_state = state_rwkv
    for l in range(depth):
        mod = _ada(cond, w_ada[l].astype(BF16), b_ada[l][None, :]).reshape(16, 6, d)

        wi = w_in[l]
        lora_w = jnp.concatenate(
            [_pad_cols(decay_a[l, 0], LANES), _pad_cols(decay_a[l, 1], LANES),
             _pad_cols(iclr_a[l, 0], LANES), _pad_cols(iclr_a[l, 1], LANES), gate_a[l]], axis=1)
        lora_w = _pad_cols(lora_w, 1024)
        w_cat = jnp.concatenate([wi[:, df:df + 3 * dr], wi[:, :df], wi[:, df + 3 * dr:], lora_w], axis=1).astype(BF16)
        col_f = 3 * dr
        col_gf = col_f + df
        col_gr = col_gf + d
        col_lora = col_gr + d

        h = _prenorm(x, g_pre_mix[l][None, :], mod, n_ctx_rows, l_lat)
        proj = _matmul(h, w_cat, 512, 1024, F32, "in_proj")

        yf_ctx = _fourier_ctx(proj, col_f, n_ctx, l_ctx, gw, n_fg)
        yf_lat = _fourier_lat(proj, col_f, n_ctx_rows, n_lat, l_lat, gw, n_fg)
        yf = jnp.concatenate([yf_ctx, yf_lat], axis=0)

        db = jnp.stack([_pad_rows(decay_b[l, z], LANES) for z in range(2)]).astype(BF16)
        ib = jnp.stack([_pad_rows(iclr_b[l, z], LANES) for z in range(2)]).astype(BF16)
        prep = _rwkv_prep(proj, col_lora, t, dr, n_ctx_rows, l_lat, conv_rkv[l], db, ib,
                          gate_b[l].astype(BF16), decay_bias[l], iclr_bias[l], k_k[l][None, :],
                          k_a[l][None, :], r_k[l].reshape(1, dr))
        scan_in, gate, bv = prep[:11], prep[11], prep[12]
        s0 = jnp.concatenate([jnp.zeros((n_ctx, 2, n_heads // 2, LANES, LANES), F32),
                              _pair_state(lat_state[:, l])], axis=0)
        per = l_lat // TM
        seq_tiles = [(b, 1) for b in range(n_ctx)] + [(n_ctx + b * per, per) for b in range(n_lat)]
        o_f, o_b, s_fin = _rwkv_scan(scan_in, s0, seq_tiles)
        states.append(_unpair_state(s_fin[:n_ctx]))
        yr = _rwkv_post(o_f, o_b, gate, bv, ln_x_w[l][None, :], ln_x_b[l][None, :])

        wr_hi = w_router[l].astype(BF16)
        wr_lo = (w_router[l] - wr_hi.astype(F32)).astype(BF16)
        xn, h2, scores = _mix(yf, yr, proj, col_gf, col_gr, x, mod, g_post_mix[l][None, :],
                              g_pre_ffn[l][None, :], w_fourier_out[l].astype(BF16),
                              w_rwkv_out[l].astype(BF16), w_out[l].astype(BF16), wr_hi, wr_lo,
                              n_ctx_rows, l_lat)

        eidx, wts = _route(scores, router_bias[l])
        slot_tok, blk_e, blk_valid, blk_next, pos, nb = _dispatch(eidx, ne)
        y = _experts(h2, slot_tok, blk_e, blk_valid, blk_next, w1_exp[l], w3_exp[l], w2_exp[l], nb)
        x = _combine(y, pos, wts, h2, xn, mod, g_post_ffn[l][None, :], w1_sh[l].astype(BF16),
                     w3_sh[l].astype(BF16), w2_sh[l].astype(BF16), n_ctx_rows, l_lat)

    y_prompt = x[:n_ctx_rows].reshape(n_ctx, l_ctx, d)
    y_sample = x[n_ctx_rows:].reshape(n_lat, l_lat, d)
    return y_prompt, y_sample, jnp.stack(states, axis=1)
```

```python
import functools

import numpy as np
import jax
import jax.numpy as jnp
from jax import lax
from jax.experimental import pallas as pl
from jax.experimental.pallas import tpu as pltpu

F32 = jnp.float32
BF16 = jnp.bfloat16

HEAD = 64
LANES = 128
TM = 256
CHUNK = 64
SCAN_PAIRS = 4
DECAY_SCALE = 0.606531
GN_EPS = 64e-5
RMS_EPS = 1e-6
N_GROUPS = 8
TOPK_GROUPS = 4
TOP_K = 8
ROUTED_SCALE = 2.5
GRID_W = 64
MOE_TM = 256
CMB_TM = 128
LOOKAHEAD = 2
N_GBUF = LOOKAHEAD + 1
VMEM_LIMIT = 56 * 1024 * 1024


def _cparams(sem):
    return pltpu.CompilerParams(dimension_semantics=sem, vmem_limit_bytes=VMEM_LIMIT)


def _dot(a, b):
    return jnp.dot(a, b, preferred_element_type=F32)


def _dot_nt(a, b):
    return lax.dot_general(a, b, (((1,), (1,)), ((), ())), preferred_element_type=F32)


def _split2(x):
    hi = x.astype(BF16)
    lo = (x - hi.astype(F32)).astype(BF16)
    return hi, lo


def _rms(x, g):
    return x * lax.rsqrt(jnp.mean(x * x, axis=-1, keepdims=True) + RMS_EPS) * g


def _pack_halves(x):
    n = x.shape[1] // 2
    bits = lax.bitcast_convert_type(x.astype(BF16).astype(F32), jnp.uint32)
    return (bits[:, n:] & jnp.uint32(0xFFFF0000)) | (bits[:, :n] >> 16)


def _unpack_halves(p):
    lo = lax.bitcast_convert_type(p << 16, F32)
    hi = lax.bitcast_convert_type(p & jnp.uint32(0xFFFF0000), F32)
    return lo, hi


def _seq_row(i, tile, n_ctx_rows, lat_len):
    t_ctx = n_ctx_rows // tile
    per = lat_len // tile
    return jnp.where(i < t_ctx, 0, 1 + (i - t_ctx) // per)


def _ada_kernel(c_ref, w_ref, b_ref, o_ref):
    c = c_ref[...]
    s = c * jax.nn.sigmoid(c)
    o_ref[...] = _dot(s.astype(BF16), w_ref[...]) + b_ref[...]


def _ada(cond, w, b):
    m, d = cond.shape
    n = w.shape[1]
    tn = 2048
    return pl.pallas_call(
        _ada_kernel,
        grid=(n // tn,),
        in_specs=[pl.BlockSpec((m, d), lambda j: (0, 0)),
                  pl.BlockSpec((d, tn), lambda j: (0, j)),
                  pl.BlockSpec((1, tn), lambda j: (0, j))],
        out_specs=pl.BlockSpec((m, tn), lambda j: (0, j)),
        out_shape=jax.ShapeDtypeStruct((m, n), F32),
        compiler_params=_cparams(("arbitrary",)),
        name="ada",
    )(cond, w, b)


def _prenorm_kernel(x_ref, g_ref, mod_ref, h_ref):
    m = mod_ref[0]
    y = _rms(x_ref[...], g_ref[...])
    h_ref[...] = (y * (1.0 + m[1:2]) + m[0:1]).astype(h_ref.dtype)


def _prenorm(x, g, mod, n_ctx_rows, lat_len):
    t, d = x.shape
    row = functools.partial(_seq_row, tile=TM, n_ctx_rows=n_ctx_rows, lat_len=lat_len)
    return pl.pallas_call(
        _prenorm_kernel,
        grid=(t // TM,),
        in_specs=[pl.BlockSpec((TM, d), lambda i: (i, 0)),
                  pl.BlockSpec((1, d), lambda i: (0, 0)),
                  pl.BlockSpec((1, 6, d), lambda i: (row(i), 0, 0))],
        out_specs=pl.BlockSpec((TM, d), lambda i: (i, 0)),
        out_shape=jax.ShapeDtypeStruct((t, d), BF16),
        compiler_params=_cparams(("arbitrary",)),
        name="prenorm",
    )(x, g, mod)


def _mm_kernel(x_ref, w_ref, o_ref):
    o_ref[...] = _dot(x_ref[...], w_ref[...]).astype(o_ref.dtype)


def _matmul(x, w, tm, tn, out_dtype, name):
    m, k = x.shape
    n = w.shape[1]
    return pl.pallas_call(
        _mm_kernel,
        grid=(n // tn, m // tm),
        in_specs=[pl.BlockSpec((tm, k), lambda j, i: (i, 0)),
                  pl.BlockSpec((k, tn), lambda j, i: (0, j))],
        out_specs=pl.BlockSpec((tm, tn), lambda j, i: (i, j)),
        out_shape=jax.ShapeDtypeStruct((m, n), out_dtype),
        compiler_params=_cparams(("arbitrary", "arbitrary")),
        name=name,
    )(x, w)


def _dft_mats(n):
    k = np.arange(n)
    ang = 2.0 * np.pi * ((k[:, None] * k[None, :]) % n) / n
    s = 1.0 / np.sqrt(n)
    return np.cos(ang) * s, np.sin(ang) * s


def _fourier_ctx_kernel(z_ref, cs_ref, ls_ref, y_ref):
    gw = z_ref.shape[1]
    ab = _dot(z_ref[...].astype(BF16), cs_ref[...])
    st = jnp.concatenate([ab[:, :gw], ab[:, gw:]], axis=0).astype(BF16)
    y_ref[...] = _dot(ls_ref[...], st).astype(y_ref.dtype)


def _fourier_ctx(proj, col0, n_seq, seq_len, gw, n_groups):
    cc, sc = _dft_mats(gw)
    cl, sl = _dft_mats(seq_len)
    cs = jnp.asarray(np.concatenate([cc, -sc], axis=1), BF16)
    ls = jnp.asarray(np.concatenate([cl, sl], axis=1), BF16)
    cb = col0 // gw
    return pl.pallas_call(
        _fourier_ctx_kernel,
        grid=(n_seq, n_groups),
        in_specs=[pl.BlockSpec((seq_len, gw), lambda b, g: (b, cb + g)),
                  pl.BlockSpec((gw, 2 * gw), lambda b, g: (0, 0)),
                  pl.BlockSpec((seq_len, 2 * seq_len), lambda b, g: (0, 0))],
        out_specs=pl.BlockSpec((seq_len, gw), lambda b, g: (b, g)),
        out_shape=jax.ShapeDtypeStruct((n_seq * seq_len, n_groups * gw), BF16),
        compiler_params=_cparams(("arbitrary", "arbitrary")),
        name="fourier_ctx",
    )(proj, cs, ls)


def _fourier_lat_a_kernel(z_ref, cs_ref, qs_ref, o_ref):
    gw = z_ref.shape[1]
    ab = _dot(z_ref[...].astype(BF16), cs_ref[...])
    for rr in range(TM // GRID_W):
        slab = ab[rr * GRID_W:(rr + 1) * GRID_W]
        swap = jnp.concatenate([slab[:, gw:], -slab[:, :gw]], axis=1)
        st = jnp.concatenate([slab, swap], axis=0).astype(BF16)
        out = _dot(qs_ref[...], st)
        o_ref[0, rr * GRID_W:(rr + 1) * GRID_W, :] = out[:, :gw].astype(o_ref.dtype)
        o_ref[1, rr * GRID_W:(rr + 1) * GRID_W, :] = out[:, gw:].astype(o_ref.dtype)


def _fourier_lat_b_kernel(x_ref, rs_ref, y_ref):
    st = jnp.concatenate([x_ref[0, 0], x_ref[1, 0]], axis=0)
    y_ref[0] = _dot(rs_ref[...], st).astype(y_ref.dtype)


def _fourier_lat(proj, col0, row0, n_seq, seq_len, gw, n_groups):
    rows = seq_len // GRID_W
    cc, sc = _dft_mats(gw)
    cq, sq = _dft_mats(GRID_W)
    cr, sr = _dft_mats(rows)
    cs = jnp.asarray(np.concatenate([cc, -sc], axis=1), BF16)
    qs = jnp.asarray(np.concatenate([cq, sq], axis=1), BF16)
    rs = jnp.asarray(np.concatenate([cr, sr], axis=1), BF16)
    cb = col0 // gw
    rb = row0 // TM
    per = seq_len // TM
    width = n_groups * gw
    ab = pl.pallas_call(
        _fourier_lat_a_kernel,
        grid=(n_seq * per, n_groups),
        in_specs=[pl.BlockSpec((TM, gw), lambda i, g: (rb + i, cb + g)),
                  pl.BlockSpec((gw, 2 * gw), lambda i, g: (0, 0)),
                  pl.BlockSpec((GRID_W, 2 * GRID_W), lambda i, g: (0, 0))],
        out_specs=pl.BlockSpec((2, TM, gw), lambda i, g: (0, i, g)),
        out_shape=jax.ShapeDtypeStruct((2, n_seq * seq_len, width), BF16),
        compiler_params=_cparams(("arbitrary", "arbitrary")),
        name="fourier_lat_cols",
    )(proj, cs, qs)
    flat = GRID_W * width
    tn = min(4096, flat)
    y = pl.pallas_call(
        _fourier_lat_b_kernel,
        grid=(n_seq, flat // tn),
        in_specs=[pl.BlockSpec((2, 1, rows, tn), lambda b, j: (0, b, 0, j)),
                  pl.BlockSpec((rows, 2 * rows), lambda b, j: (0, 0))],
        out_specs=pl.BlockSpec((1, rows, tn), lambda b, j: (b, 0, j)),
        out_shape=jax.ShapeDtypeStruct((n_seq, rows, flat), BF16),
        compiler_params=_cparams(("arbitrary", "arbitrary")),
        name="fourier_lat_rows",
    )(ab.reshape(2, n_seq, rows, flat), rs)
    return y.reshape(n_seq * seq_len, width)


def _group_sum(x, ones_ref):
    hi, lo = _split2(x)
    return _dot(hi, ones_ref[...]) + _dot(lo, ones_ref[...])


def _rwkv_prep_kernel(u_ref, up_ref, un_ref, lora_ref, conv_ref, db_ref, ib_ref, gb_ref,
                      dbias_ref, ibias_ref, kkw_ref, ka_ref, rk_ref, ones_ref, tri_ref,
                      at0_o, rt0_o, bt0_o, kt0_o, pt0_o, at1_o, rt1_o, bt1_o, kt1_o, pt1_o, v_o, g_o, bv_o,
                      *, t_ctx, per):
    i = pl.program_id(0)
    dr = v_o.shape[0] * LANES
    j = i - t_ctx
    is_first = jnp.logical_or(i < t_ctx, lax.rem(j, per) == 0)
    is_last = jnp.logical_or(i < t_ctx, lax.rem(j, per) == per - 1)
    u = u_ref[...]
    rows = lax.broadcasted_iota(jnp.int32, (TM, 1), 0)
    prev_row = jnp.where(is_first, 0.0, up_ref[7:8, :])
    next_row = jnp.where(is_last, 0.0, un_ref[0:1, :])
    u_prev = jnp.where(rows == 0, prev_row, pltpu.roll(u, 1, 0))
    u_next = jnp.where(rows == TM - 1, next_row, pltpu.roll(u, TM - 1, 0))
    cw = conv_ref[...]
    rkv = u_prev * cw[0:1] + u * cw[1:2] + u_next * cw[2:3]
    r = rkv[:, :dr]
    k = rkv[:, dr:2 * dr]
    v = rkv[:, 2 * dr:]

    lora = lora_ref[...]
    dbias = dbias_ref[...]
    ibias = ibias_ref[...]
    lws, avs = [], []
    for z in range(2):
        dz = jnp.tanh(lora[:, z * LANES:(z + 1) * LANES]).astype(BF16)
        lws.append(-DECAY_SCALE * jax.nn.sigmoid(dbias[z:z + 1] + _dot(dz, db_ref[z])))
        iz = lora[:, (2 + z) * LANES:(3 + z) * LANES].astype(BF16)
        avs.append(jax.nn.sigmoid(ibias[z:z + 1] + _dot(iz, ib_ref[z])))
    gate = _dot(jax.nn.sigmoid(lora[:, 4 * LANES:6 * LANES]).astype(BF16), gb_ref[...])

    kkr = k * kkw_ref[...]
    ka = ka_ref[...]
    k0 = k * (1.0 + (avs[0] - 1.0) * ka)
    k1 = k * (1.0 + (avs[1] - 1.0) * ka)
    bon_in = r * (k0 + k1) * rk_ref[...]
    g_o[...] = gate
    kds = (k0, k1)
    outs = ((at0_o, rt0_o, bt0_o, kt0_o, pt0_o), (at1_o, rt1_o, bt1_o, kt1_o, pt1_o))
    n_chunks = TM // CHUNK
    for p in range(dr // LANES):
        sl = slice(p * LANES, (p + 1) * LANES)
        kk_p = kkr[:, sl]
        ss = _group_sum(kk_p * kk_p, ones_ref)
        kk_p = kk_p / jnp.maximum(jnp.sqrt(ss), 1e-12)
        r_p = r[:, sl]
        v_o[p] = v[:, sl].astype(BF16)
        bv_o[:, sl] = _group_sum(bon_in[:, sl], ones_ref) * v[:, sl]
        for z in range(2):
            at_o, rt_o, bt_o, kt_o, pt_o = outs[z]
            lw = lws[z][:, sl]
            h0 = lw.astype(BF16)
            r1 = lw - h0.astype(F32)
            h1 = r1.astype(BF16)
            h2 = (r1 - h1.astype(F32)).astype(BF16)
            cum = _dot(tri_ref[z], h0) + _dot(tri_ref[z], h1) + _dot(tri_ref[z], h2)
            p_in = jnp.exp(cum)
            p_inv = jnp.exp(-cum)
            at_o[p] = (-kk_p * jnp.exp(cum - lw)).astype(BF16)
            rt_o[p] = (r_p * p_in).astype(BF16)
            bt_o[p] = (kk_p * avs[z][:, sl] * p_inv).astype(BF16)
            kt_o[p] = (kds[z][:, sl] * p_inv).astype(BF16)
            for ch in range(n_chunks):
                row = ch * CHUNK + (0 if z == 1 else CHUNK - 1)
                pt_o[p, ch * 8:(ch + 1) * 8, :] = jnp.broadcast_to(p_in[row:row + 1, :], (8, LANES))


def _head_block_ones(scale):
    m = np.zeros((LANES, LANES), np.float32)
    m[:HEAD, :HEAD] = scale
    m[HEAD:, HEAD:] = scale
    return jnp.asarray(m, BF16)


def _rwkv_prep(proj, lora_col, t, dr, n_ctx_rows, lat_len, conv, db, ib, gb, dbias, ibias, kkw, ka, rk):
    t_ctx = n_ctx_rows // TM
    per = lat_len // TM
    n_tiles = t // TM
    npair = dr // LANES
    w3 = 3 * dr
    lb = lora_col // 1024
    nb8 = t // 8
    pair = jax.ShapeDtypeStruct((npair, t, LANES), BF16)
    ptot = jax.ShapeDtypeStruct((npair, t // 8, LANES), F32)
    flat = jax.ShapeDtypeStruct((t, dr), F32)
    full = lambda a: pl.BlockSpec(a.shape, lambda i, _n=a.ndim: (0,) * _n)
    ones = _head_block_ones(1.0)
    idx = np.arange(TM)
    same = (idx[:, None] // CHUNK) == (idx[None, :] // CHUNK)
    tri = jnp.asarray(np.stack([same & (idx[None, :] <= idx[:, None]),
                                same & (idx[None, :] >= idx[:, None])]).astype(np.float32), BF16)
    pair_spec = pl.BlockSpec((npair, TM, LANES), lambda i: (0, i, 0))
    ptot_spec = pl.BlockSpec((npair, TM // 8, LANES), lambda i: (0, i, 0))
    flat_spec = pl.BlockSpec((TM, dr), lambda i: (i, 0))
    dir_specs = [pair_spec] * 4 + [ptot_spec]
    dir_shapes = [pair] * 4 + [ptot]
    return pl.pallas_call(
        functools.partial(_rwkv_prep_kernel, t_ctx=t_ctx, per=per),
        grid=(n_tiles,),
        in_specs=[pl.BlockSpec((TM, w3), lambda i: (i, 0)),
                  pl.BlockSpec((8, w3), lambda i: (jnp.maximum(i * (TM // 8) - 1, 0), 0)),
                  pl.BlockSpec((8, w3), lambda i: (jnp.minimum((i + 1) * (TM // 8), nb8 - 1), 0)),
                  pl.BlockSpec((TM, 1024), lambda i: (i, lb)),
                  full(conv), full(db), full(ib), full(gb), full(dbias), full(ibias),
                  full(kkw), full(ka), full(rk), full(ones), full(tri)],
        out_specs=dir_specs * 2 + [pair_spec, flat_spec, flat_spec],
        out_shape=dir_shapes * 2 + [pair, flat, flat],
        compiler_params=_cparams(("arbitrary",)),
        name="rwkv_prep",
    )(proj, proj, proj, proj, conv, db, ib, gb, dbias, ibias, kkw, ka, rk, ones, tri)


def _scan_chunks(chains, m0):
    c = chains[0][0].shape[0]
    n2 = 2 * c
    zero = jnp.zeros_like(chains[0][0])
    rng = range(len(chains))

    def two_heads(x):
        return jnp.concatenate([jnp.where(m0, x, zero), jnp.where(m0, zero, x)], axis=0)

    vb16 = [two_heads(ch[4]) for ch in chains]
    lhs = [jnp.concatenate([two_heads(ch[0]), two_heads(ch[1])], axis=0) for ch in chains]
    rhs = [jnp.concatenate([two_heads(ch[2]), two_heads(ch[3])], axis=0) for ch in chains]
    gram = [_dot_nt(lhs[i], rhs[i]) for i in rng]
    a_s = [_dot_nt(lhs[i], chains[i][6].astype(BF16)) for i in rng]
    l_ka = [jnp.where(chains[i][7], gram[i][:n2, n2:], 0.0).astype(BF16) for i in rng]
    lp = [jnp.where(chains[i][7], gram[i][:n2, :n2], 0.0).astype(BF16) for i in rng]
    m_rk = [jnp.concatenate([jnp.where(chains[i][8], gram[i][n2:, :n2], 0.0),
                             jnp.where(chains[i][8], gram[i][n2:, n2:], 0.0)], axis=1).astype(BF16)
            for i in rng]
    x = [a_s[i][:n2] + _dot(l_ka[i], vb16[i]) for i in rng]
    n_iter = int(np.log2(c))
    for it in range(n_iter):
        if it < n_iter - 1:
            both = [_dot(lp[i], jnp.concatenate([lp[i], x[i].astype(BF16)], axis=1)) for i in rng]
            lp = [both[i][:, :n2].astype(BF16) for i in rng]
            x = [x[i] + both[i][:, n2:] for i in rng]
        else:
            x = [x[i] + _dot(lp[i], x[i].astype(BF16)) for i in rng]
    o_bd = [a_s[i][n2:] + _dot(m_rk[i], jnp.concatenate([x[i].astype(BF16), vb16[i]], axis=0)) for i in rng]
    uvt = [jnp.concatenate([x[i].T, vb16[i].astype(F32).T], axis=1).astype(BF16) for i in rng]
    s_new = [(chains[i][6] + _dot(uvt[i], rhs[i])) * chains[i][5] for i in rng]
    return [o_bd[i][:c] + o_bd[i][c:] for i in rng], s_new


def _rwkv_scan_kernel(ft_ref, bt_ref, sq_ref, first_ref, last_ref,
                      atf, rtf, btf, ktf, ptf, vf, atb, rtb, btb, ktb, ptb, vb, s0_ref,
                      of_ref, ob_ref, sout_ref, s_scr):
    it = pl.program_id(1)
    npp = s_scr.shape[1]

    @pl.when(first_ref[it] == 1)
    def _():
        s_scr[...] = s0_ref[0]

    c = CHUNK
    n2 = 2 * c
    ri = lax.broadcasted_iota(jnp.int32, (n2, n2), 0)
    ci = lax.broadcasted_iota(jnp.int32, (n2, n2), 1)
    same = (ri // c) == (ci // c)
    rp = ri % c
    cp = ci % c
    m0 = lax.broadcasted_iota(jnp.int32, (c, LANES), 1) < HEAD
    n_chunks = TM // c
    masks = ((jnp.logical_and(same, cp < rp), jnp.logical_and(same, cp <= rp)),
             (jnp.logical_and(same, cp > rp), jnp.logical_and(same, cp >= rp)))
    dirs = ((atf, rtf, btf, ktf, ptf, vf, of_ref), (atb, rtb, btb, ktb, ptb, vb, ob_ref))

    def body(ch, carry):
        chains, dests = [], []
        for q in range(npp):
            for z in range(2):
                a_r, r_r, b_r, k_r, p_r, v_r, o_r = dirs[z]
                cc = ch if z == 0 else n_chunks - 1 - ch
                sl = pl.ds(pl.multiple_of(cc * c, c), c)
                p_tot = p_r[q, pl.ds(pl.multiple_of(cc * 8, 8), 1), :]
                chains.append((a_r[q, sl, :], r_r[q, sl, :], b_r[q, sl, :], k_r[q, sl, :], v_r[q, sl, :],
                               p_tot, s_scr[z, q], masks[z][0], masks[z][1]))
                dests.append((o_r, q, sl, z))
        outs, states = _scan_chunks(chains, m0)
        for (o_r, q, sl, z), o, s_new in zip(dests, outs, states):
            o_r[q, sl, :] = o
            s_scr[z, q] = s_new
        return carry

    lax.fori_loop(0, n_chunks, body, 0)

    @pl.when(last_ref[it] == 1)
    def _():
        sout_ref[0] = s_scr[...]


def _rwkv_scan(prep, s0, seq_tiles):
    at0, rt0, bt0, kt0, pt0, at1, rt1, bt1, kt1, pt1, v = prep
    npair, t, _ = v.shape
    npp = SCAN_PAIRS
    ft, bt, sq, first, last = [], [], [], [], []
    for si, (t0, n) in enumerate(seq_tiles):
        for j in range(n):
            ft.append(t0 + j)
            bt.append(t0 + n - 1 - j)
            sq.append(si)
            first.append(int(j == 0))
            last.append(int(j == n - 1))
    tabs = [jnp.asarray(np.array(a, np.int32)) for a in (ft, bt, sq, first, last)]
    fwd = pl.BlockSpec((npp, TM, LANES), lambda p, i, ft, bt, sq, fi, la: (p, ft[i], 0))
    bwd = pl.BlockSpec((npp, TM, LANES), lambda p, i, ft, bt, sq, fi, la: (p, bt[i], 0))
    fwd_p = pl.BlockSpec((npp, TM // 8, LANES), lambda p, i, ft, bt, sq, fi, la: (p, ft[i], 0))
    bwd_p = pl.BlockSpec((npp, TM // 8, LANES), lambda p, i, ft, bt, sq, fi, la: (p, bt[i], 0))
    st = pl.BlockSpec((1, 2, npp, LANES, LANES), lambda p, i, ft, bt, sq, fi, la: (sq[i], 0, p, 0, 0))
    o_shape = jax.ShapeDtypeStruct((npair, t, LANES), F32)
    grid_spec = pltpu.PrefetchScalarGridSpec(
        num_scalar_prefetch=5,
        grid=(npair // npp, len(ft)),
        in_specs=[fwd] * 4 + [fwd_p, fwd] + [bwd] * 4 + [bwd_p, bwd] + [st],
        out_specs=[fwd, bwd, st],
        scratch_shapes=[pltpu.VMEM((2, npp, LANES, LANES), F32)],
    )
    return pl.pallas_call(
        _rwkv_scan_kernel,
        grid_spec=grid_spec,
        out_shape=[o_shape, o_shape, jax.ShapeDtypeStruct(s0.shape, F32)],
        compiler_params=_cparams(("arbitrary", "arbitrary")),
        name="rwkv_scan",
    )(*tabs, at0, rt0, bt0, kt0, pt0, v, at1, rt1, bt1, kt1, pt1, v, s0)


def _rwkv_post_kernel(of_ref, ob_ref, g_ref, bv_ref, lnw_ref, lnb_ref, avg_ref, y_ref):
    for p in range(of_ref.shape[0]):
        sl = slice(p * LANES, (p + 1) * LANES)
        o = of_ref[p] + ob_ref[p]
        hi, lo = _split2(o)
        mu = _dot(hi, avg_ref[...]) + _dot(lo, avg_ref[...])
        d = o - mu
        hi, lo = _split2(d * d)
        var = _dot(hi, avg_ref[...]) + _dot(lo, avg_ref[...])
        y = d * lax.rsqrt(var + GN_EPS) * lnw_ref[:, sl] + lnb_ref[:, sl]
        y_ref[:, sl] = ((y + bv_ref[:, sl]) * g_ref[:, sl]).astype(y_ref.dtype)


def _rwkv_post(o_f, o_b, g, bv, lnw, lnb):
    npair, t, _ = o_f.shape
    dr = npair * LANES
    avg = _head_block_ones(1.0 / HEAD)
    pair_spec = pl.BlockSpec((npair, TM, LANES), lambda i: (0, i, 0))
    flat_spec = pl.BlockSpec((TM, dr), lambda i: (i, 0))
    vec = pl.BlockSpec((1, dr), lambda i: (0, 0))
    return pl.pallas_call(
        _rwkv_post_kernel,
        grid=(t // TM,),
        in_specs=[pair_spec, pair_spec, flat_spec, flat_spec, vec, vec,
                  pl.BlockSpec((LANES, LANES), lambda i: (0, 0))],
        out_specs=flat_spec,
        out_shape=jax.ShapeDtypeStruct((t, dr), BF16),
        compiler_params=_cparams(("arbitrary",)),
        name="rwkv_post",
    )(o_f, o_b, g, bv, lnw, lnb, avg)


def _mix_kernel(yf_ref, yr_ref, gf_ref, gr_ref, x_ref, mod_ref, gpost_ref, gpre_ref,
                wf_ref, wr_ref, wo_ref, wrh_ref, wrl_ref, xn_ref, h2_ref, sc_ref):
    a = _dot(yf_ref[...], wf_ref[...])
    b = _dot(yr_ref[...], wr_ref[...])
    merged = jax.nn.sigmoid(gf_ref[...]) * a + jax.nn.sigmoid(gr_ref[...]) * b
    m = _dot(merged.astype(BF16), wo_ref[...])
    mod = mod_ref[0]
    xn = x_ref[...] + mod[2:3] * _rms(m, gpost_ref[...])
    xn_ref[...] = xn
    h2 = _rms(xn, gpre_ref[...]) * (1.0 + mod[4:5]) + mod[3:4]
    h2_ref[...] = _pack_halves(h2)
    hi, lo = _split2(h2)
    logits = _dot(hi, wrh_ref[...]) + (_dot(hi, wrl_ref[...]) + _dot(lo, wrh_ref[...]))
    sc_ref[...] = jax.nn.sigmoid(logits)


def _mix(yf, yr, proj, gf_col, gr_col, x, mod, gpost, gpre, wf, wr, wo, wrh, wrl, n_ctx_rows, lat_len):
    t, d = x.shape
    tm = TM
    df = yf.shape[1]
    ne = wrh.shape[1]
    row = functools.partial(_seq_row, tile=tm, n_ctx_rows=n_ctx_rows, lat_len=lat_len)
    full = lambda a: pl.BlockSpec(a.shape, lambda i, _n=a.ndim: (0,) * _n, pipeline_mode=pl.Buffered(1))
    rowspec = lambda w, cb=0: pl.BlockSpec((tm, w), lambda i: (i, cb))
    return pl.pallas_call(
        _mix_kernel,
        grid=(t // tm,),
        in_specs=[rowspec(df), rowspec(df), rowspec(d, gf_col // d), rowspec(d, gr_col // d), rowspec(d),
                  pl.BlockSpec((1, 6, d), lambda i: (row(i), 0, 0)),
                  full(gpost), full(gpre), full(wf), full(wr), full(wo), full(wrh), full(wrl)],
        out_specs=[rowspec(d), rowspec(d // 2), rowspec(ne)],
        out_shape=[jax.ShapeDtypeStruct((t, d), F32), jax.ShapeDtypeStruct((t, d // 2), jnp.uint32),
                   jax.ShapeDtypeStruct((t, ne), F32)],
        compiler_params=_cparams(("arbitrary",)),
        name="mix_out",
    )(yf, yr, proj, proj, x, mod, gpost, gpre, wf, wr, wo, wrh, wrl)


def _route_kernel(sc_ref, bias_ref, idx_ref, wt_ref):
    sc = sc_ref[...]
    tm, ne = sc.shape
    per = ne // N_GROUPS
    lane_i = lax.broadcasted_iota(jnp.int32, (tm, ne), 1)
    lane = lane_i.astype(F32)
    gid = lane_i // per
    sel = sc + bias_ref[...]
    neg = -jnp.inf
    far = float(ne)

    def first_max(x):
        m = jnp.max(x, axis=-1, keepdims=True)
        i = jnp.min(jnp.where(x == m, lane, far), axis=-1, keepdims=True)
        return m, i

    gs = []
    for g in range(N_GROUPS):
        xg = jnp.where(gid == g, sel, neg)
        m1, i1 = first_max(xg)
        m2 = jnp.max(jnp.where(lane == i1, neg, xg), axis=-1, keepdims=True)
        gs.append(m1 + m2)
    keep = jnp.zeros((tm, ne), jnp.bool_)
    for g in range(N_GROUPS):
        rank = jnp.zeros((tm, 1), F32)
        for h in range(N_GROUPS):
            if h == g:
                continue
            ahead = (gs[h] >= gs[g]) if h < g else (gs[h] > gs[g])
            rank = rank + jnp.where(ahead, 1.0, 0.0)
        keep = jnp.logical_or(keep, jnp.logical_and(rank < TOPK_GROUPS, gid == g))
    cur = jnp.where(keep, sel, neg)
    lane_k = lax.broadcasted_iota(jnp.int32, (tm, TOP_K), 1)
    idx = jnp.zeros((tm, TOP_K), F32)
    wts = jnp.zeros((tm, TOP_K), F32)
    tot = jnp.zeros((tm, 1), F32)
    for k in range(TOP_K):
        _, ik = first_max(cur)
        hit = lane == ik
        wk = jnp.sum(jnp.where(hit, sc, 0.0), axis=-1, keepdims=True)
        cur = jnp.where(hit, neg, cur)
        idx = jnp.where(lane_k == k, ik, idx)
        wts = jnp.where(lane_k == k, wk, wts)
        tot = tot + wk
    idx_ref[...] = idx.astype(jnp.int32)
    wt_ref[...] = wts / tot * ROUTED_SCALE


def _route(scores, router_bias):
    t, ne = scores.shape
    spec = pl.BlockSpec((TM, TOP_K), lambda i: (i, 0))
    return pl.pallas_call(
        _route_kernel,
        grid=(t // TM,),
        in_specs=[pl.BlockSpec((TM, ne), lambda i: (i, 0)), pl.BlockSpec((1, ne), lambda i: (0, 0))],
        out_specs=[spec, spec],
        out_shape=[jax.ShapeDtypeStruct((t, TOP_K), jnp.int32), jax.ShapeDtypeStruct((t, TOP_K), F32)],
        compiler_params=_cparams(("arbitrary",)),
        name="route",
    )(scores, router_bias[None, :])


def _dispatch(eidx, ne):
    t, k = eidx.shape
    a = t * k
    nb = a // MOE_TM + ne
    eid = eidx.reshape(-1)
    counts = jnp.sum((eid[:, None] == jnp.arange(ne, dtype=jnp.int32)[None, :]).astype(jnp.int32), axis=0)
    padc = (-counts) % MOE_TM
    q = jnp.arange(MOE_TM, dtype=jnp.int32)
    pad_valid = q[None, :] < padc[:, None]
    pad_e = jnp.where(pad_valid, jnp.arange(ne, dtype=jnp.int32)[:, None], ne).reshape(-1)
    key_real = eid * (2 * a) + jnp.arange(a, dtype=jnp.int32)
    key_pad = pad_e * (2 * a) + a + jnp.arange(ne * MOE_TM, dtype=jnp.int32) % a
    keys = jnp.sort(jnp.concatenate([key_real, key_pad]))
    slot_e = keys // (2 * a)
    rem = keys % (2 * a)
    is_real = rem < a
    slot_a = jnp.where(is_real, rem, 0)
    slot_tok = slot_a // k
    blk_e = slot_e[::MOE_TM]
    blk_valid = (blk_e < ne).astype(jnp.int32)
    blk_e = jnp.minimum(blk_e, ne - 1).astype(jnp.int32)
    cand = jnp.where(counts > 0, jnp.arange(ne, dtype=jnp.int32), ne)
    suffix_min = lax.cummin(cand[::-1])[::-1]
    next_e = jnp.concatenate([suffix_min[1:], jnp.full((1,), ne, jnp.int32)])
    blk_next = next_e[blk_e]
    blk_next = jnp.where(blk_next < ne, blk_next, -1).astype(jnp.int32)
    n_slots = nb * MOE_TM
    sort_key = jnp.where(is_real, rem, a)
    _, pos = lax.sort((sort_key, jnp.arange(n_slots, dtype=jnp.int32)), num_keys=1)
    pos = pos[:a].reshape(t, k)
    return slot_tok.astype(jnp.int32), blk_e, blk_valid, blk_next, pos.astype(jnp.int32), nb


def _expert_kernel(be_ref, bv_ref, nx_ref, tok_ref, h_hbm, w1_hbm, w3_hbm, w2_hbm, y_ref,
                   xbuf0, xbuf1, xbuf2, wf1, wf3, wf2, w1b, w3b, w2b, gsem, wsem, *, nb):
    j = pl.program_id(0)
    jm = jnp.maximum(j - LOOKAHEAD, 0)
    issue_ok = jnp.logical_and(j < nb, bv_ref[jnp.minimum(j, nb - 1)] == 1)
    comp_ok = jnp.logical_and(j >= LOOKAHEAD, bv_ref[jm] == 1)
    par = lax.rem(j, N_GBUF)
    xbufs = (xbuf0, xbuf1, xbuf2)

    def weight_copies(e):
        return (pltpu.make_async_copy(w1_hbm.at[e], wf1, wsem.at[0]),
                pltpu.make_async_copy(w3_hbm.at[e], wf3, wsem.at[1]),
                pltpu.make_async_copy(w2_hbm.at[e], wf2, wsem.at[2]))

    @pl.when(jnp.logical_and(j == 0, bv_ref[0] == 1))
    def _():
        for cp in weight_copies(be_ref[0]):
            cp.start(priority=1)

    run_start = jnp.logical_and(comp_ok, jnp.logical_or(jm == 0, be_ref[jm] != be_ref[jnp.maximum(jm - 1, 0)]))

    @pl.when(run_start)
    def _():
        for cp in weight_copies(be_ref[jm]):
            cp.wait()
        w1b[...] = wf1[...].astype(BF16)
        w3b[...] = wf3[...].astype(BF16)
        w2b[...] = wf2[...].astype(BF16)

        @pl.when(nx_ref[jm] >= 0)
        def _():
            for cp in weight_copies(nx_ref[jm]):
                cp.start(priority=1)

    def gather_wait(p):
        for _ in range(MOE_TM):
            pltpu.make_async_copy(h_hbm.at[pl.ds(0, 1), :], xbufs[p].at[pl.ds(0, 1), :], gsem.at[p]).wait()

    def gather_start(p):
        for rr in range(MOE_TM):
            pltpu.make_async_copy(h_hbm.at[pl.ds(tok_ref[0, 0, rr], 1), :],
                                  xbufs[p].at[pl.ds(rr, 1), :], gsem.at[p]).start(priority=0)

    def compute(p):
        x = jnp.concatenate(_unpack_halves(xbufs[p][...]), axis=1).astype(BF16)
        a = _dot(x, w1b[...])
        b = _dot(x, w3b[...])
        hb = (a * jax.nn.sigmoid(a) * b).astype(BF16)
        y_ref[...] = _pack_halves(_dot(hb, w2b[...]))

    for p in range(N_GBUF):
        q = (p - LOOKAHEAD) % N_GBUF

        @pl.when(jnp.logical_and(jnp.logical_and(issue_ok, comp_ok), par == p))
        def _(p=p, q=q):
            gather_wait(q)
            gather_start(p)
            compute(q)

        @pl.when(jnp.logical_and(jnp.logical_and(jnp.logical_not(issue_ok), comp_ok), par == p))
        def _(q=q):
            gather_wait(q)
            compute(q)

    for p in range(LOOKAHEAD):
        @pl.when(jnp.logical_and(issue_ok, j == p))
        def _(p=p):
            gather_start(p)

    @pl.when(jnp.logical_and(j >= LOOKAHEAD, bv_ref[jm] == 0))
    def _():
        y_ref[...] = jnp.zeros_like(y_ref)


def _experts(h2p, slot_tok, blk_e, blk_valid, blk_next, w1, w3, w2, nb):
    ne, d, de = w1.shape
    dp = d // 2
    tok3 = slot_tok.reshape(nb, 1, MOE_TM)
    grid_spec = pltpu.PrefetchScalarGridSpec(
        num_scalar_prefetch=3,
        grid=(nb + LOOKAHEAD,),
        in_specs=[pl.BlockSpec((1, 1, MOE_TM), lambda j, be, bv, nx: (jnp.minimum(j, nb - 1), 0, 0),
                               memory_space=pltpu.SMEM),
                  pl.BlockSpec(memory_space=pl.ANY), pl.BlockSpec(memory_space=pl.ANY),
                  pl.BlockSpec(memory_space=pl.ANY), pl.BlockSpec(memory_space=pl.ANY)],
        out_specs=pl.BlockSpec((MOE_TM, dp), lambda j, be, bv, nx: (jnp.maximum(j - LOOKAHEAD, 0), 0)),
        scratch_shapes=[pltpu.VMEM((MOE_TM, dp), jnp.uint32)] * N_GBUF + [
                        pltpu.VMEM((d, de), F32), pltpu.VMEM((d, de), F32), pltpu.VMEM((de, d), F32),
                        pltpu.VMEM((d, de), BF16), pltpu.VMEM((d, de), BF16), pltpu.VMEM((de, d), BF16),
                        pltpu.SemaphoreType.DMA((N_GBUF,)), pltpu.SemaphoreType.DMA((3,))],
    )
    return pl.pallas_call(
        functools.partial(_expert_kernel, nb=nb),
        grid_spec=grid_spec,
        out_shape=jax.ShapeDtypeStruct((nb * MOE_TM, dp), jnp.uint32),
        compiler_params=_cparams(("arbitrary",)),
        name="experts",
    )(blk_e, blk_valid, blk_next, tok3, h2p, w1, w3, w2)


def _combine_kernel(pos_ref, y_hbm, h2_ref, xn_ref, wt_ref, mod_ref, gpost_ref,
                    w1_ref, w3_ref, w2_ref, o_ref, ybuf0, ybuf1, ybuf2, sem, *, n_tiles, k):
    j = pl.program_id(0)
    par = lax.rem(j, N_GBUF)
    n_rows = CMB_TM * k
    ybufs = (ybuf0, ybuf1, ybuf2)

    def gather_wait(p):
        for _ in range(n_rows):
            pltpu.make_async_copy(y_hbm.at[pl.ds(0, 1), :], ybufs[p].at[pl.ds(0, 1), :], sem.at[p]).wait()

    def gather_start(p):
        for rr in range(n_rows):
            pltpu.make_async_copy(y_hbm.at[pl.ds(pos_ref[0, 0, rr], 1), :],
                                  ybufs[p].at[pl.ds(rr, 1), :], sem.at[p]).start(priority=rr % 2)

    def compute(p):
        x = jnp.concatenate(_unpack_halves(h2_ref[...]), axis=1).astype(BF16)
        a = _dot(x, w1_ref[...])
        b = _dot(x, w3_ref[...])
        f = _dot((a * jax.nn.sigmoid(a) * b).astype(BF16), w2_ref[...])
        wt = wt_ref[...]
        f_lo = jnp.zeros((CMB_TM, x.shape[1] // 2), F32)
        f_hi = f_lo
        for kk in range(k):
            lo, hi = _unpack_halves(ybufs[p][kk * CMB_TM:(kk + 1) * CMB_TM, :])
            f_lo = f_lo + wt[:, kk:kk + 1] * lo
            f_hi = f_hi + wt[:, kk:kk + 1] * hi
        f = f + jnp.concatenate([f_lo, f_hi], axis=1)
        mod = mod_ref[0]
        o_ref[...] = xn_ref[...] + mod[5:6] * _rms(f, gpost_ref[...])

    for p in range(LOOKAHEAD):
        @pl.when(j == p)
        def _(p=p):
            gather_start(p)

    for p in range(N_GBUF):
        q = (p - LOOKAHEAD) % N_GBUF

        @pl.when(jnp.logical_and(jnp.logical_and(j >= LOOKAHEAD, j < n_tiles), par == p))
        def _(p=p, q=q):
            gather_wait(q)
            gather_start(p)
            compute(q)

    for jj in range(n_tiles, n_tiles + LOOKAHEAD):
        @pl.when(j == jj)
        def _(jj=jj):
            gather_wait((jj - LOOKAHEAD) % N_GBUF)
            compute((jj - LOOKAHEAD) % N_GBUF)


def _combine(y, pos, wts, h2p, xn, mod, gpost, w1, w3, w2, n_ctx_rows, lat_len):
    t, d = xn.shape
    dp = d // 2
    k = pos.shape[1]
    n_tiles = t // CMB_TM
    pos3 = pos.reshape(n_tiles, CMB_TM, k).transpose(0, 2, 1).reshape(n_tiles, 1, CMB_TM * k)
    row = functools.partial(_seq_row, tile=CMB_TM, n_ctx_rows=n_ctx_rows, lat_len=lat_len)
    prev = lambda j: jnp.maximum(j - LOOKAHEAD, 0)
    full = lambda a: pl.BlockSpec(a.shape, lambda j, _n=a.ndim: (0,) * _n, pipeline_mode=pl.Buffered(1))
    rowspec = lambda w: pl.BlockSpec((CMB_TM, w), lambda j: (prev(j), 0))
    return pl.pallas_call(
        functools.partial(_combine_kernel, n_tiles=n_tiles, k=k),
        grid=(n_tiles + LOOKAHEAD,),
        in_specs=[pl.BlockSpec((1, 1, CMB_TM * k), lambda j: (jnp.minimum(j, n_tiles - 1), 0, 0),
                               memory_space=pltpu.SMEM),
                  pl.BlockSpec(memory_space=pl.ANY),
                  rowspec(dp), rowspec(d), rowspec(k),
                  pl.BlockSpec((1, 6, d), lambda j: (row(prev(j)), 0, 0)),
                  full(gpost), full(w1), full(w3), full(w2)],
        out_specs=rowspec(d),
        out_shape=jax.ShapeDtypeStruct((t, d), F32),
        scratch_shapes=[pltpu.VMEM((CMB_TM * k, dp), jnp.uint32)] * N_GBUF + [pltpu.SemaphoreType.DMA((N_GBUF,))],
        compiler_params=_cparams(("arbitrary",)),
        name="combine",
    )(pos3, y, h2p, xn, wts, mod, gpost, w1, w3, w2)


def _pad_cols(w, n):
    return jnp.pad(w, ((0, 0), (0, n - w.shape[1])))


def _pad_rows(w, n):
    return jnp.pad(w, ((0, n - w.shape[0]), (0, 0)))


def _pair_state(s):
    n, nd, h, hv, hk = s.shape
    s = s.reshape(n, nd, h // 2, 2, hv, hk)
    z = jnp.zeros_like(s[:, :, :, 0])
    top = jnp.concatenate([s[:, :, :, 0], z], axis=-1)
    bot = jnp.concatenate([z, s[:, :, :, 1]], axis=-1)
    return jnp.concatenate([top, bot], axis=-2)


def _unpair_state(s):
    n, nd, hp = s.shape[:3]
    a = s[..., :HEAD, :HEAD]
    b = s[..., HEAD:, HEAD:]
    return jnp.stack([a, b], axis=3).reshape(n, nd, hp * 2, HEAD, HEAD)


def kernel(x_prompt, x_sample, state_rwkv, c, c_ctx, w_ada, b_ada, g_pre_mix, g_post_mix, g_pre_ffn, g_post_ffn, w_in, conv_rkv, decay_a, decay_b, decay_bias, iclr_a, iclr_b, iclr_bias, gate_a, gate_b, k_k, k_a, r_k, ln_x_w, ln_x_b, w_fourier_out, w_rwkv_out, w_out, w_router, router_bias, w1_exp, w3_exp, w2_exp, w1_sh, w3_sh, w2_sh):
    n_ctx, l_ctx, d = x_prompt.shape
    n_lat, l_lat, _ = x_sample.shape
    depth = w_ada.shape[0]
    assert l_ctx == TM and l_lat % TM == 0 and l_lat % GRID_W == 0
    n_ctx_rows = n_ctx * l_ctx
    t = n_ctx_rows + n_lat * l_lat
    dr = k_k.shape[1]
    df = w_fourier_out.shape[1]
    n_fg = 4
    gw = df // n_fg
    n_heads = dr // HEAD
    ne = w_router.shape[2]
    lora_r = decay_a.shape[3]
    assert lora_r <= LANES and gate_a.shape[2] == 2 * LANES

    x = jnp.concatenate([x_prompt.reshape(n_ctx_rows, d), x_sample.reshape(n_lat * l_lat, d)], axis=0)
    cond = jnp.concatenate([c_ctx[None, :], c, jnp.zeros((16 - 1 - n_lat, d), F32)], axis=0)
    states = []
    lat_state = state_rwkv
    for l in range(depth):
        mod = _ada(cond, w_ada[l].astype(BF16), b_ada[l][None, :]).reshape(16, 6, d)

        wi = w_in[l]
        lora_w = jnp.concatenate(
            [_pad_cols(decay_a[l, 0], LANES), _pad_cols(decay_a[l, 1], LANES),
             _pad_cols(iclr_a[l, 0], LANES), _pad_cols(iclr_a[l, 1], LANES), gate_a[l]], axis=1)
        lora_w = _pad_cols(lora_w, 1024)
        w_cat = jnp.concatenate([wi[:, df:df + 3 * dr], wi[:, :df], wi[:, df + 3 * dr:], lora_w], axis=1).astype(BF16)
        col_f = 3 * dr
        col_gf = col_f + df
        col_gr = col_gf + d
        col_lora = col_gr + d

        h = _prenorm(x, g_pre_mix[l][None, :], mod, n_ctx_rows, l_lat)
        proj = _matmul(h, w_cat, 512, 1024, F32, "in_proj")

        yf_ctx = _fourier_ctx(proj, col_f, n_ctx, l_ctx, gw, n_fg)
        yf_lat = _fourier_lat(proj, col_f, n_ctx_rows, n_lat, l_lat, gw, n_fg)
        yf = jnp.concatenate([yf_ctx, yf_lat], axis=0)

        db = jnp.stack([_pad_rows(decay_b[l, z], LANES) for z in range(2)]).astype(BF16)
        ib = jnp.stack([_pad_rows(iclr_b[l, z], LANES) for z in range(2)]).astype(BF16)
        prep = _rwkv_prep(proj, col_lora, t, dr, n_ctx_rows, l_lat, conv_rkv[l], db, ib,
                          gate_b[l].astype(BF16), decay_bias[l], iclr_bias[l], k_k[l][None, :],
                          k_a[l][None, :], r_k[l].reshape(1, dr))
        scan_in, gate, bv = prep[:11], prep[11], prep[12]
        s0 = jnp.concatenate([jnp.zeros((n_ctx, 2, n_heads // 2, LANES, LANES), F32),
                              _pair_state(lat_state[:, l])], axis=0)
        per = l_lat // TM
        seq_tiles = [(b, 1) for b in range(n_ctx)] + [(n_ctx + b * per, per) for b in range(n_lat)]
        o_f, o_b, s_fin = _rwkv_scan(scan_in, s0, seq_tiles)
        states.append(_unpair_state(s_fin[:n_ctx]))
        yr = _rwkv_post(o_f, o_b, gate, bv, ln_x_w[l][None, :], ln_x_b[l][None, :])

        wr_hi = w_router[l].astype(BF16)
        wr_lo = (w_router[l] - wr_hi.astype(F32)).astype(BF16)
        xn, h2, scores = _mix(yf, yr, proj, col_gf, col_gr, x, mod, g_post_mix[l][None, :],
                              g_pre_ffn[l][None, :], w_fourier_out[l].astype(BF16),
                              w_rwkv_out[l].astype(BF16), w_out[l].astype(BF16), wr_hi, wr_lo,
                              n_ctx_rows, l_lat)

        eidx, wts = _route(scores, router_bias[l])
        slot_tok, blk_e, blk_valid, blk_next, pos, nb = _dispatch(eidx, ne)
        y = _experts(h2, slot_tok, blk_e, blk_valid, blk_next, w1_exp[l], w3_exp[l], w2_exp[l], nb)
        x = _combine(y, pos, wts, h2, xn, mod, g_post_ffn[l][None, :], w1_sh[l].astype(BF16),
                     w3_sh[l].astype(BF16), w2_sh[l].astype(BF16), n_ctx_rows, l_lat)

    y_prompt = x[:n_ctx_rows].reshape(n_ctx, l_ctx, d)
    y_sample = x[n_ctx_rows:].reshape(n_lat, l_lat, d)
    return y_prompt, y_sample, jnp.stack(states, axis=1)
```

```python
import functools

import numpy as np
import jax
import jax.numpy as jnp
from jax import lax
from jax.experimental import pallas as pl
from jax.experimental.pallas import tpu as pltpu

F32 = jnp.float32
BF16 = jnp.bfloat16

HEAD = 64
LANES = 128
TM = 256
CHUNK = 64
SCAN_PAIRS = 4
DECAY_SCALE = 0.606531
GN_EPS = 64e-5
RMS_EPS = 1e-6
N_GROUPS = 8
TOPK_GROUPS = 4
TOP_K = 8
ROUTED_SCALE = 2.5
GRID_W = 64
MOE_TM = 256
CMB_TM = 128
LOOKAHEAD = 3
N_GBUF = LOOKAHEAD + 1
VMEM_LIMIT = 56 * 1024 * 1024


def _cparams(sem):
    return pltpu.CompilerParams(dimension_semantics=sem, vmem_limit_bytes=VMEM_LIMIT)


def _dot(a, b):
    return jnp.dot(a, b, preferred_element_type=F32)


def _dot_nt(a, b):
    return lax.dot_general(a, b, (((1,), (1,)), ((), ())), preferred_element_type=F32)


def _split2(x):
    hi = x.astype(BF16)
    lo = (x - hi.astype(F32)).astype(BF16)
    return hi, lo


def _rms(x, g):
    return x * lax.rsqrt(jnp.mean(x * x, axis=-1, keepdims=True) + RMS_EPS) * g


def _pack_halves(x):
    n = x.shape[1] // 2
    bits = lax.bitcast_convert_type(x.astype(BF16).astype(F32), jnp.uint32)
    return (bits[:, n:] & jnp.uint32(0xFFFF0000)) | (bits[:, :n] >> 16)


def _unpack_halves(p):
    lo = lax.bitcast_convert_type(p << 16, F32)
    hi = lax.bitcast_convert_type(p & jnp.uint32(0xFFFF0000), F32)
    return lo, hi


def _seq_row(i, tile, n_ctx_rows, lat_len):
    t_ctx = n_ctx_rows // tile
    per = lat_len // tile
    return jnp.where(i < t_ctx, 0, 1 + (i - t_ctx) // per)


def _ada_kernel(c_ref, w_ref, b_ref, o_ref):
    c = c_ref[...]
    s = c * jax.nn.sigmoid(c)
    o_ref[...] = _dot(s.astype(BF16), w_ref[...]) + b_ref[...]


def _ada(cond, w, b):
    m, d = cond.shape
    n = w.shape[1]
    tn = 2048
    return pl.pallas_call(
        _ada_kernel,
        grid=(n // tn,),
        in_specs=[pl.BlockSpec((m, d), lambda j: (0, 0)),
                  pl.BlockSpec((d, tn), lambda j: (0, j)),
                  pl.BlockSpec((1, tn), lambda j: (0, j))],
        out_specs=pl.BlockSpec((m, tn), lambda j: (0, j)),
        out_shape=jax.ShapeDtypeStruct((m, n), F32),
        compiler_params=_cparams(("arbitrary",)),
        name="ada",
    )(cond, w, b)


def _two_source_specs(tile, d, t_ctx):
    return [pl.BlockSpec((tile, d), lambda i: (jnp.minimum(i, t_ctx - 1), 0)),
            pl.BlockSpec((tile, d), lambda i: (jnp.maximum(i - t_ctx, 0), 0))]


def _prenorm_kernel(xc_ref, xl_ref, g_ref, mod_ref, h_ref, *, t_ctx):
    m = mod_ref[0]
    x = jnp.where(pl.program_id(0) < t_ctx, xc_ref[...], xl_ref[...])
    y = _rms(x, g_ref[...])
    h_ref[...] = (y * (1.0 + m[1:2]) + m[0:1]).astype(h_ref.dtype)


def _prenorm(xc, xl, g, mod, lat_len):
    n_ctx_rows, d = xc.shape
    t = n_ctx_rows + xl.shape[0]
    t_ctx = n_ctx_rows // TM
    row = functools.partial(_seq_row, tile=TM, n_ctx_rows=n_ctx_rows, lat_len=lat_len)
    return pl.pallas_call(
        functools.partial(_prenorm_kernel, t_ctx=t_ctx),
        grid=(t // TM,),
        in_specs=_two_source_specs(TM, d, t_ctx) + [
                  pl.BlockSpec((1, d), lambda i: (0, 0)),
                  pl.BlockSpec((1, 6, d), lambda i: (row(i), 0, 0))],
        out_specs=pl.BlockSpec((TM, d), lambda i: (i, 0)),
        out_shape=jax.ShapeDtypeStruct((t, d), BF16),
        compiler_params=_cparams(("arbitrary",)),
        name="prenorm",
    )(xc, xl, g, mod)


def _mm_kernel(x_ref, w_ref, o_ref):
    o_ref[...] = _dot(x_ref[...], w_ref[...]).astype(o_ref.dtype)


def _matmul(x, w, tm, tn, out_dtype, name):
    m, k = x.shape
    n = w.shape[1]
    return pl.pallas_call(
        _mm_kernel,
        grid=(n // tn, m // tm),
        in_specs=[pl.BlockSpec((tm, k), lambda j, i: (i, 0)),
                  pl.BlockSpec((k, tn), lambda j, i: (0, j))],
        out_specs=pl.BlockSpec((tm, tn), lambda j, i: (i, j)),
        out_shape=jax.ShapeDtypeStruct((m, n), out_dtype),
        compiler_params=_cparams(("arbitrary", "arbitrary")),
        name=name,
    )(x, w)


def _dft_mats(n):
    k = np.arange(n)
    ang = 2.0 * np.pi * ((k[:, None] * k[None, :]) % n) / n
    s = 1.0 / np.sqrt(n)
    return np.cos(ang) * s, np.sin(ang) * s


def _fourier_ctx_kernel(z_ref, cs_ref, ls_ref, y_ref):
    gw = z_ref.shape[1]
    ab = _dot(z_ref[...].astype(BF16), cs_ref[...])
    st = jnp.concatenate([ab[:, :gw], ab[:, gw:]], axis=0).astype(BF16)
    y_ref[...] = _dot(ls_ref[...], st).astype(y_ref.dtype)


def _fourier_ctx(proj, col0, n_seq, seq_len, gw, n_groups):
    cc, sc = _dft_mats(gw)
    cl, sl = _dft_mats(seq_len)
    cs = jnp.asarray(np.concatenate([cc, -sc], axis=1), BF16)
    ls = jnp.asarray(np.concatenate([cl, sl], axis=1), BF16)
    cb = col0 // gw
    return pl.pallas_call(
        _fourier_ctx_kernel,
        grid=(n_seq, n_groups),
        in_specs=[pl.BlockSpec((seq_len, gw), lambda b, g: (b, cb + g)),
                  pl.BlockSpec((gw, 2 * gw), lambda b, g: (0, 0)),
                  pl.BlockSpec((seq_len, 2 * seq_len), lambda b, g: (0, 0))],
        out_specs=pl.BlockSpec((seq_len, gw), lambda b, g: (b, g)),
        out_shape=jax.ShapeDtypeStruct((n_seq * seq_len, n_groups * gw), BF16),
        compiler_params=_cparams(("arbitrary", "arbitrary")),
        name="fourier_ctx",
    )(proj, cs, ls)


def _fourier_lat_a_kernel(z_ref, cs_ref, qs_ref, o_ref):
    gw = z_ref.shape[1]
    ab = _dot(z_ref[...].astype(BF16), cs_ref[...])
    for rr in range(TM // GRID_W):
        slab = ab[rr * GRID_W:(rr + 1) * GRID_W]
        swap = jnp.concatenate([slab[:, gw:], -slab[:, :gw]], axis=1)
        st = jnp.concatenate([slab, swap], axis=0).astype(BF16)
        out = _dot(qs_ref[...], st)
        o_ref[0, rr * GRID_W:(rr + 1) * GRID_W, :] = out[:, :gw].astype(o_ref.dtype)
        o_ref[1, rr * GRID_W:(rr + 1) * GRID_W, :] = out[:, gw:].astype(o_ref.dtype)


def _fourier_lat_b_kernel(x_ref, rs_ref, y_ref):
    st = jnp.concatenate([x_ref[0, 0], x_ref[1, 0]], axis=0)
    y_ref[0] = _dot(rs_ref[...], st).astype(y_ref.dtype)


def _fourier_lat(proj, col0, row0, n_seq, seq_len, gw, n_groups):
    rows = seq_len // GRID_W
    cc, sc = _dft_mats(gw)
    cq, sq = _dft_mats(GRID_W)
    cr, sr = _dft_mats(rows)
    cs = jnp.asarray(np.concatenate([cc, -sc], axis=1), BF16)
    qs = jnp.asarray(np.concatenate([cq, sq], axis=1), BF16)
    rs = jnp.asarray(np.concatenate([cr, sr], axis=1), BF16)
    cb = col0 // gw
    rb = row0 // TM
    per = seq_len // TM
    width = n_groups * gw
    ab = pl.pallas_call(
        _fourier_lat_a_kernel,
        grid=(n_seq * per, n_groups),
        in_specs=[pl.BlockSpec((TM, gw), lambda i, g: (rb + i, cb + g)),
                  pl.BlockSpec((gw, 2 * gw), lambda i, g: (0, 0)),
                  pl.BlockSpec((GRID_W, 2 * GRID_W), lambda i, g: (0, 0))],
        out_specs=pl.BlockSpec((2, TM, gw), lambda i, g: (0, i, g)),
        out_shape=jax.ShapeDtypeStruct((2, n_seq * seq_len, width), BF16),
        compiler_params=_cparams(("arbitrary", "arbitrary")),
        name="fourier_lat_cols",
    )(proj, cs, qs)
    flat = GRID_W * width
    tn = min(4096, flat)
    y = pl.pallas_call(
        _fourier_lat_b_kernel,
        grid=(n_seq, flat // tn),
        in_specs=[pl.BlockSpec((2, 1, rows, tn), lambda b, j: (0, b, 0, j)),
                  pl.BlockSpec((rows, 2 * rows), lambda b, j: (0, 0))],
        out_specs=pl.BlockSpec((1, rows, tn), lambda b, j: (b, 0, j)),
        out_shape=jax.ShapeDtypeStruct((n_seq, rows, flat), BF16),
        compiler_params=_cparams(("arbitrary", "arbitrary")),
        name="fourier_lat_rows",
    )(ab.reshape(2, n_seq, rows, flat), rs)
    return y.reshape(n_seq * seq_len, width)


def _group_sum(x, ones_ref):
    hi, lo = _split2(x)
    return _dot(hi, ones_ref[...]) + _dot(lo, ones_ref[...])


def _rwkv_prep_kernel(u_ref, up_ref, un_ref, lora_ref, conv_ref, db_ref, ib_ref, gb_ref,
                      dbias_ref, ibias_ref, kkw_ref, ka_ref, rk_ref, ones_ref, tri_ref,
                      at0_o, rt0_o, bt0_o, kt0_o, pt0_o, at1_o, rt1_o, bt1_o, kt1_o, pt1_o, v_o, g_o, bv_o,
                      *, t_ctx, per):
    i = pl.program_id(0)
    dr = v_o.shape[0] * LANES
    j = i - t_ctx
    is_first = jnp.logical_or(i < t_ctx, lax.rem(j, per) == 0)
    is_last = jnp.logical_or(i < t_ctx, lax.rem(j, per) == per - 1)
    u = u_ref[...]
    rows = lax.broadcasted_iota(jnp.int32, (TM, 1), 0)
    prev_row = jnp.where(is_first, 0.0, up_ref[7:8, :])
    next_row = jnp.where(is_last, 0.0, un_ref[0:1, :])
    u_prev = jnp.where(rows == 0, prev_row, pltpu.roll(u, 1, 0))
    u_next = jnp.where(rows == TM - 1, next_row, pltpu.roll(u, TM - 1, 0))
    cw = conv_ref[...]
    rkv = u_prev * cw[0:1] + u * cw[1:2] + u_next * cw[2:3]
    r = rkv[:, :dr]
    k = rkv[:, dr:2 * dr]
    v = rkv[:, 2 * dr:]

    lora = lora_ref[...]
    dbias = dbias_ref[...]
    ibias = ibias_ref[...]
    lws, avs = [], []
    for z in range(2):
        dz = jnp.tanh(lora[:, z * LANES:(z + 1) * LANES]).astype(BF16)
        lws.append(-DECAY_SCALE * jax.nn.sigmoid(dbias[z:z + 1] + _dot(dz, db_ref[z])))
        iz = lora[:, (2 + z) * LANES:(3 + z) * LANES].astype(BF16)
        avs.append(jax.nn.sigmoid(ibias[z:z + 1] + _dot(iz, ib_ref[z])))
    gate = _dot(jax.nn.sigmoid(lora[:, 4 * LANES:6 * LANES]).astype(BF16), gb_ref[...])

    kkr = k * kkw_ref[...]
    ka = ka_ref[...]
    k0 = k * (1.0 + (avs[0] - 1.0) * ka)
    k1 = k * (1.0 + (avs[1] - 1.0) * ka)
    bon_in = r * (k0 + k1) * rk_ref[...]
    g_o[...] = gate
    kds = (k0, k1)
    outs = ((at0_o, rt0_o, bt0_o, kt0_o, pt0_o), (at1_o, rt1_o, bt1_o, kt1_o, pt1_o))
    n_chunks = TM // CHUNK
    for p in range(dr // LANES):
        sl = slice(p * LANES, (p + 1) * LANES)
        kk_p = kkr[:, sl]
        ss = _group_sum(kk_p * kk_p, ones_ref)
        kk_p = kk_p / jnp.maximum(jnp.sqrt(ss), 1e-12)
        r_p = r[:, sl]
        v_o[p] = v[:, sl].astype(BF16)
        bv_o[:, sl] = _group_sum(bon_in[:, sl], ones_ref) * v[:, sl]
        for z in range(2):
            at_o, rt_o, bt_o, kt_o, pt_o = outs[z]
            lw = lws[z][:, sl]
            h0 = lw.astype(BF16)
            r1 = lw - h0.astype(F32)
            h1 = r1.astype(BF16)
            h2 = (r1 - h1.astype(F32)).astype(BF16)
            cum = _dot(tri_ref[z], h0) + _dot(tri_ref[z], h1) + _dot(tri_ref[z], h2)
            p_in = jnp.exp(cum)
            p_inv = jnp.exp(-cum)
            at_o[p] = (-kk_p * jnp.exp(cum - lw)).astype(BF16)
            rt_o[p] = (r_p * p_in).astype(BF16)
            bt_o[p] = (kk_p * avs[z][:, sl] * p_inv).astype(BF16)
            kt_o[p] = (kds[z][:, sl] * p_inv).astype(BF16)
            for ch in range(n_chunks):
                row = ch * CHUNK + (0 if z == 1 else CHUNK - 1)
                pt_o[p, ch * 8:(ch + 1) * 8, :] = jnp.broadcast_to(p_in[row:row + 1, :], (8, LANES))


def _head_block_ones(scale):
    m = np.zeros((LANES, LANES), np.float32)
    m[:HEAD, :HEAD] = scale
    m[HEAD:, HEAD:] = scale
    return jnp.asarray(m, BF16)


def _rwkv_prep(proj, lora_col, t, dr, n_ctx_rows, lat_len, conv, db, ib, gb, dbias, ibias, kkw, ka, rk):
    t_ctx = n_ctx_rows // TM
    per = lat_len // TM
    n_tiles = t // TM
    npair = dr // LANES
    w3 = 3 * dr
    lb = lora_col // 1024
    nb8 = t // 8
    pair = jax.ShapeDtypeStruct((npair, t, LANES), BF16)
    ptot = jax.ShapeDtypeStruct((npair, t // 8, LANES), F32)
    flat = jax.ShapeDtypeStruct((t, dr), F32)
    full = lambda a: pl.BlockSpec(a.shape, lambda i, _n=a.ndim: (0,) * _n)
    ones = _head_block_ones(1.0)
    idx = np.arange(TM)
    same = (idx[:, None] // CHUNK) == (idx[None, :] // CHUNK)
    tri = jnp.asarray(np.stack([same & (idx[None, :] <= idx[:, None]),
                                same & (idx[None, :] >= idx[:, None])]).astype(np.float32), BF16)
    pair_spec = pl.BlockSpec((npair, TM, LANES), lambda i: (0, i, 0))
    ptot_spec = pl.BlockSpec((npair, TM // 8, LANES), lambda i: (0, i, 0))
    flat_spec = pl.BlockSpec((TM, dr), lambda i: (i, 0))
    dir_specs = [pair_spec] * 4 + [ptot_spec]
    dir_shapes = [pair] * 4 + [ptot]
    return pl.pallas_call(
        functools.partial(_rwkv_prep_kernel, t_ctx=t_ctx, per=per),
        grid=(n_tiles,),
        in_specs=[pl.BlockSpec((TM, w3), lambda i: (i, 0)),
                  pl.BlockSpec((8, w3), lambda i: (jnp.maximum(i * (TM // 8) - 1, 0), 0)),
                  pl.BlockSpec((8, w3), lambda i: (jnp.minimum((i + 1) * (TM // 8), nb8 - 1), 0)),
                  pl.BlockSpec((TM, 1024), lambda i: (i, lb)),
                  full(conv), full(db), full(ib), full(gb), full(dbias), full(ibias),
                  full(kkw), full(ka), full(rk), full(ones), full(tri)],
        out_specs=dir_specs * 2 + [pair_spec, flat_spec, flat_spec],
        out_shape=dir_shapes * 2 + [pair, flat, flat],
        compiler_params=_cparams(("arbitrary",)),
        name="rwkv_prep",
    )(proj, proj, proj, proj, conv, db, ib, gb, dbias, ibias, kkw, ka, rk, ones, tri)


def _scan_chunks(chains, m0):
    c = chains[0][0].shape[0]
    n2 = 2 * c
    zero = jnp.zeros_like(chains[0][0])
    rng = range(len(chains))

    def two_heads(x):
        return jnp.concatenate([jnp.where(m0, x, zero), jnp.where(m0, zero, x)], axis=0)

    vb16 = [two_heads(ch[4]) for ch in chains]
    lhs = [jnp.concatenate([two_heads(ch[0]), two_heads(ch[1])], axis=0) for ch in chains]
    rhs = [jnp.concatenate([two_heads(ch[2]), two_heads(ch[3])], axis=0) for ch in chains]
    gram = [_dot_nt(lhs[i], rhs[i]) for i in rng]
    a_s = [_dot_nt(lhs[i], chains[i][6].astype(BF16)) for i in rng]
    l_ka = [jnp.where(chains[i][7], gram[i][:n2, n2:], 0.0).astype(BF16) for i in rng]
    lp = [jnp.where(chains[i][7], gram[i][:n2, :n2], 0.0).astype(BF16) for i in rng]
    m_rk = [jnp.concatenate([jnp.where(chains[i][8], gram[i][n2:, :n2], 0.0),
                             jnp.where(chains[i][8], gram[i][n2:, n2:], 0.0)], axis=1).astype(BF16)
            for i in rng]
    x = [a_s[i][:n2] + _dot(l_ka[i], vb16[i]) for i in rng]
    n_iter = int(np.log2(c))
    for it in range(n_iter):
        if it < n_iter - 1:
            both = [_dot(lp[i], jnp.concatenate([lp[i], x[i].astype(BF16)], axis=1)) for i in rng]
            lp = [both[i][:, :n2].astype(BF16) for i in rng]
            x = [x[i] + both[i][:, n2:] for i in rng]
        else:
            x = [x[i] + _dot(lp[i], x[i].astype(BF16)) for i in rng]
    o_bd = [a_s[i][n2:] + _dot(m_rk[i], jnp.concatenate([x[i].astype(BF16), vb16[i]], axis=0)) for i in rng]
    uvt = [jnp.concatenate([x[i].T, vb16[i].astype(F32).T], axis=1).astype(BF16) for i in rng]
    s_new = [(chains[i][6] + _dot(uvt[i], rhs[i])) * chains[i][5] for i in rng]
    return [o_bd[i][:c] + o_bd[i][c:] for i in rng], s_new


def _rwkv_scan_kernel(ft_ref, bt_ref, sq_ref, first_ref, last_ref,
                      atf, rtf, btf, ktf, ptf, vf, atb, rtb, btb, ktb, ptb, vb, s0_ref,
                      of_ref, ob_ref, sout_ref, s_scr):
    it = pl.program_id(1)
    npp = s_scr.shape[1]

    @pl.when(first_ref[it] == 1)
    def _():
        s_scr[...] = s0_ref[0]

    c = CHUNK
    n2 = 2 * c
    ri = lax.broadcasted_iota(jnp.int32, (n2, n2), 0)
    ci = lax.broadcasted_iota(jnp.int32, (n2, n2), 1)
    same = (ri // c) == (ci // c)
    rp = ri % c
    cp = ci % c
    m0 = lax.broadcasted_iota(jnp.int32, (c, LANES), 1) < HEAD
    n_chunks = TM // c
    masks = ((jnp.logical_and(same, cp < rp), jnp.logical_and(same, cp <= rp)),
             (jnp.logical_and(same, cp > rp), jnp.logical_and(same, cp >= rp)))
    dirs = ((atf, rtf, btf, ktf, ptf, vf, of_ref), (atb, rtb, btb, ktb, ptb, vb, ob_ref))

    def body(ch, carry):
        chains, dests = [], []
        for q in range(npp):
            for z in range(2):
                a_r, r_r, b_r, k_r, p_r, v_r, o_r = dirs[z]
                cc = ch if z == 0 else n_chunks - 1 - ch
                sl = pl.ds(pl.multiple_of(cc * c, c), c)
                p_tot = p_r[q, pl.ds(pl.multiple_of(cc * 8, 8), 1), :]
                chains.append((a_r[q, sl, :], r_r[q, sl, :], b_r[q, sl, :], k_r[q, sl, :], v_r[q, sl, :],
                               p_tot, s_scr[z, q], masks[z][0], masks[z][1]))
                dests.append((o_r, q, sl, z))
        outs, states = _scan_chunks(chains, m0)
        for (o_r, q, sl, z), o, s_new in zip(dests, outs, states):
            o_r[q, sl, :] = o
            s_scr[z, q] = s_new
        return carry

    lax.fori_loop(0, n_chunks, body, 0)

    @pl.when(last_ref[it] == 1)
    def _():
        sout_ref[0] = s_scr[...]


def _rwkv_scan(prep, s0, seq_tiles):
    at0, rt0, bt0, kt0, pt0, at1, rt1, bt1, kt1, pt1, v = prep
    npair, t, _ = v.shape
    npp = SCAN_PAIRS
    ft, bt, sq, first, last = [], [], [], [], []
    for si, (t0, n) in enumerate(seq_tiles):
        for j in range(n):
            ft.append(t0 + j)
            bt.append(t0 + n - 1 - j)
            sq.append(si)
            first.append(int(j == 0))
            last.append(int(j == n - 1))
    tabs = [jnp.asarray(np.array(a, np.int32)) for a in (ft, bt, sq, first, last)]
    fwd = pl.BlockSpec((npp, TM, LANES), lambda p, i, ft, bt, sq, fi, la: (p, ft[i], 0))
    bwd = pl.BlockSpec((npp, TM, LANES), lambda p, i, ft, bt, sq, fi, la: (p, bt[i], 0))
    fwd_p = pl.BlockSpec((npp, TM // 8, LANES), lambda p, i, ft, bt, sq, fi, la: (p, ft[i], 0))
    bwd_p = pl.BlockSpec((npp, TM // 8, LANES), lambda p, i, ft, bt, sq, fi, la: (p, bt[i], 0))
    st = pl.BlockSpec((1, 2, npp, LANES, LANES), lambda p, i, ft, bt, sq, fi, la: (sq[i], 0, p, 0, 0))
    o_shape = jax.ShapeDtypeStruct((npair, t, LANES), F32)
    grid_spec = pltpu.PrefetchScalarGridSpec(
        num_scalar_prefetch=5,
        grid=(npair // npp, len(ft)),
        in_specs=[fwd] * 4 + [fwd_p, fwd] + [bwd] * 4 + [bwd_p, bwd] + [st],
        out_specs=[fwd, bwd, st],
        scratch_shapes=[pltpu.VMEM((2, npp, LANES, LANES), F32)],
    )
    return pl.pallas_call(
        _rwkv_scan_kernel,
        grid_spec=grid_spec,
        out_shape=[o_shape, o_shape, jax.ShapeDtypeStruct(s0.shape, F32)],
        compiler_params=_cparams(("arbitrary", "arbitrary")),
        name="rwkv_scan",
    )(*tabs, at0, rt0, bt0, kt0, pt0, v, at1, rt1, bt1, kt1, pt1, v, s0)


def _rwkv_post_kernel(of_ref, ob_ref, g_ref, bv_ref, lnw_ref, lnb_ref, avg_ref, y_ref):
    for p in range(of_ref.shape[0]):
        sl = slice(p * LANES, (p + 1) * LANES)
        o = of_ref[p] + ob_ref[p]
        hi, lo = _split2(o)
        mu = _dot(hi, avg_ref[...]) + _dot(lo, avg_ref[...])
        d = o - mu
        hi, lo = _split2(d * d)
        var = _dot(hi, avg_ref[...]) + _dot(lo, avg_ref[...])
        y = d * lax.rsqrt(var + GN_EPS) * lnw_ref[:, sl] + lnb_ref[:, sl]
        y_ref[:, sl] = ((y + bv_ref[:, sl]) * g_ref[:, sl]).astype(y_ref.dtype)


def _rwkv_post(o_f, o_b, g, bv, lnw, lnb):
    npair, t, _ = o_f.shape
    dr = npair * LANES
    avg = _head_block_ones(1.0 / HEAD)
    pair_spec = pl.BlockSpec((npair, TM, LANES), lambda i: (0, i, 0))
    flat_spec = pl.BlockSpec((TM, dr), lambda i: (i, 0))
    vec = pl.BlockSpec((1, dr), lambda i: (0, 0))
    return pl.pallas_call(
        _rwkv_post_kernel,
        grid=(t // TM,),
        in_specs=[pair_spec, pair_spec, flat_spec, flat_spec, vec, vec,
                  pl.BlockSpec((LANES, LANES), lambda i: (0, 0))],
        out_specs=flat_spec,
        out_shape=jax.ShapeDtypeStruct((t, dr), BF16),
        compiler_params=_cparams(("arbitrary",)),
        name="rwkv_post",
    )(o_f, o_b, g, bv, lnw, lnb, avg)


def _mix_kernel(yf_ref, yr_ref, gf_ref, gr_ref, xc_ref, xl_ref, mod_ref, gpost_ref, gpre_ref,
                wf_ref, wr_ref, wo_ref, wrh_ref, wrl_ref, xn_ref, h2_ref, sc_ref, *, t_ctx):
    a = _dot(yf_ref[...], wf_ref[...])
    b = _dot(yr_ref[...], wr_ref[...])
    merged = jax.nn.sigmoid(gf_ref[...]) * a + jax.nn.sigmoid(gr_ref[...]) * b
    m = _dot(merged.astype(BF16), wo_ref[...])
    mod = mod_ref[0]
    x = jnp.where(pl.program_id(0) < t_ctx, xc_ref[...], xl_ref[...])
    xn = x + mod[2:3] * _rms(m, gpost_ref[...])
    xn_ref[...] = xn
    h2 = _rms(xn, gpre_ref[...]) * (1.0 + mod[4:5]) + mod[3:4]
    h2_ref[...] = _pack_halves(h2)
    hi, lo = _split2(h2)
    logits = _dot(hi, wrh_ref[...]) + (_dot(hi, wrl_ref[...]) + _dot(lo, wrh_ref[...]))
    sc_ref[...] = jax.nn.sigmoid(logits)


def _mix(yf, yr, proj, gf_col, gr_col, xc, xl, mod, gpost, gpre, wf, wr, wo, wrh, wrl, lat_len):
    n_ctx_rows, d = xc.shape
    t = n_ctx_rows + xl.shape[0]
    tm = TM
    t_ctx = n_ctx_rows // tm
    df = yf.shape[1]
    ne = wrh.shape[1]
    row = functools.partial(_seq_row, tile=tm, n_ctx_rows=n_ctx_rows, lat_len=lat_len)
    full = lambda a: pl.BlockSpec(a.shape, lambda i, _n=a.ndim: (0,) * _n, pipeline_mode=pl.Buffered(1))
    rowspec = lambda w, cb=0: pl.BlockSpec((tm, w), lambda i: (i, cb))
    return pl.pallas_call(
        functools.partial(_mix_kernel, t_ctx=t_ctx),
        grid=(t // tm,),
        in_specs=[rowspec(df), rowspec(df), rowspec(d, gf_col // d), rowspec(d, gr_col // d)]
                 + _two_source_specs(tm, d, t_ctx) + [
                  pl.BlockSpec((1, 6, d), lambda i: (row(i), 0, 0)),
                  full(gpost), full(gpre), full(wf), full(wr), full(wo), full(wrh), full(wrl)],
        out_specs=[rowspec(d), rowspec(d // 2), rowspec(ne)],
        out_shape=[jax.ShapeDtypeStruct((t, d), F32), jax.ShapeDtypeStruct((t, d // 2), jnp.uint32),
                   jax.ShapeDtypeStruct((t, ne), F32)],
        compiler_params=_cparams(("arbitrary",)),
        name="mix_out",
    )(yf, yr, proj, proj, xc, xl, mod, gpost, gpre, wf, wr, wo, wrh, wrl)


def _route_kernel(sc_ref, bias_ref, idx_ref, wt_ref):
    sc = sc_ref[...]
    tm, ne = sc.shape
    per = ne // N_GROUPS
    lane_i = lax.broadcasted_iota(jnp.int32, (tm, ne), 1)
    lane = lane_i.astype(F32)
    gid = lane_i // per
    sel = sc + bias_ref[...]
    neg = -jnp.inf
    far = float(ne)

    def first_max(x):
        m = jnp.max(x, axis=-1, keepdims=True)
        i = jnp.min(jnp.where(x == m, lane, far), axis=-1, keepdims=True)
        return m, i

    gs = []
    for g in range(N_GROUPS):
        xg = jnp.where(gid == g, sel, neg)
        m1, i1 = first_max(xg)
        m2 = jnp.max(jnp.where(lane == i1, neg, xg), axis=-1, keepdims=True)
        gs.append(m1 + m2)
    keep = jnp.zeros((tm, ne), jnp.bool_)
    for g in range(N_GROUPS):
        rank = jnp.zeros((tm, 1), F32)
        for h in range(N_GROUPS):
            if h == g:
                continue
            ahead = (gs[h] >= gs[g]) if h < g else (gs[h] > gs[g])
            rank = rank + jnp.where(ahead, 1.0, 0.0)
        keep = jnp.logical_or(keep, jnp.logical_and(rank < TOPK_GROUPS, gid == g))
    cur = jnp.where(keep, sel, neg)
    lane_k = lax.broadcasted_iota(jnp.int32, (tm, TOP_K), 1)
    idx = jnp.zeros((tm, TOP_K), F32)
    wts = jnp.zeros((tm, TOP_K), F32)
    tot = jnp.zeros((tm, 1), F32)
    for k in range(TOP_K):
        _, ik = first_max(cur)
        hit = lane == ik
        wk = jnp.sum(jnp.where(hit, sc, 0.0), axis=-1, keepdims=True)
        cur = jnp.where(hit, neg, cur)
        idx = jnp.where(lane_k == k, ik, idx)
        wts = jnp.where(lane_k == k, wk, wts)
        tot = tot + wk
    idx_ref[...] = idx.astype(jnp.int32)
    wt_ref[...] = wts / tot * ROUTED_SCALE


def _route(scores, router_bias):
    t, ne = scores.shape
    spec = pl.BlockSpec((TM, TOP_K), lambda i: (i, 0))
    return pl.pallas_call(
        _route_kernel,
        grid=(t // TM,),
        in_specs=[pl.BlockSpec((TM, ne), lambda i: (i, 0)), pl.BlockSpec((1, ne), lambda i: (0, 0))],
        out_specs=[spec, spec],
        out_shape=[jax.ShapeDtypeStruct((t, TOP_K), jnp.int32), jax.ShapeDtypeStruct((t, TOP_K), F32)],
        compiler_params=_cparams(("arbitrary",)),
        name="route",
    )(scores, router_bias[None, :])


def _dispatch(eidx, ne):
    t, k = eidx.shape
    a = t * k
    nb = a // MOE_TM + ne
    eid = eidx.reshape(-1)
    counts = jnp.sum((eid[:, None] == jnp.arange(ne, dtype=jnp.int32)[None, :]).astype(jnp.int32), axis=0)
    padc = (-counts) % MOE_TM
    q = jnp.arange(MOE_TM, dtype=jnp.int32)
    pad_valid = q[None, :] < padc[:, None]
    pad_e = jnp.where(pad_valid, jnp.arange(ne, dtype=jnp.int32)[:, None], ne).reshape(-1)
    key_real = eid * (2 * a) + jnp.arange(a, dtype=jnp.int32)
    key_pad = pad_e * (2 * a) + a + jnp.arange(ne * MOE_TM, dtype=jnp.int32) % a
    keys = jnp.sort(jnp.concatenate([key_real, key_pad]))
    slot_e = keys // (2 * a)
    rem = keys % (2 * a)
    is_real = rem < a
    slot_a = jnp.where(is_real, rem, 0)
    slot_tok = slot_a // k
    blk_e = slot_e[::MOE_TM]
    blk_valid = (blk_e < ne).astype(jnp.int32)
    blk_e = jnp.minimum(blk_e, ne - 1).astype(jnp.int32)
    cand = jnp.where(counts > 0, jnp.arange(ne, dtype=jnp.int32), ne)
    suffix_min = lax.cummin(cand[::-1])[::-1]
    next_e = jnp.concatenate([suffix_min[1:], jnp.full((1,), ne, jnp.int32)])
    blk_next = next_e[blk_e]
    blk_next = jnp.where(blk_next < ne, blk_next, -1).astype(jnp.int32)
    n_slots = nb * MOE_TM
    sort_key = jnp.where(is_real, rem, a)
    _, pos = lax.sort((sort_key, jnp.arange(n_slots, dtype=jnp.int32)), num_keys=1)
    pos = pos[:a].reshape(t, k)
    return slot_tok.astype(jnp.int32), blk_e, blk_valid, blk_next, pos.astype(jnp.int32), nb


def _expert_kernel(be_ref, bv_ref, nx_ref, tok_ref, h_hbm, w1_hbm, w3_hbm, w2_hbm, y_ref, *scratch, nb):
    j = pl.program_id(0)
    jm = jnp.maximum(j - LOOKAHEAD, 0)
    issue_ok = jnp.logical_and(j < nb, bv_ref[jnp.minimum(j, nb - 1)] == 1)
    comp_ok = jnp.logical_and(j >= LOOKAHEAD, bv_ref[jm] == 1)
    par = lax.rem(j, N_GBUF)
    xbufs = scratch[:N_GBUF]
    wf1, wf3, wf2, w1b, w3b, w2b, gsem, wsem = scratch[N_GBUF:]

    def weight_copies(e):
        return (pltpu.make_async_copy(w1_hbm.at[e], wf1, wsem.at[0]),
                pltpu.make_async_copy(w3_hbm.at[e], wf3, wsem.at[1]),
                pltpu.make_async_copy(w2_hbm.at[e], wf2, wsem.at[2]))

    @pl.when(jnp.logical_and(j == 0, bv_ref[0] == 1))
    def _():
        for cp in weight_copies(be_ref[0]):
            cp.start(priority=1)

    run_start = jnp.logical_and(comp_ok, jnp.logical_or(jm == 0, be_ref[jm] != be_ref[jnp.maximum(jm - 1, 0)]))

    @pl.when(run_start)
    def _():
        for cp in weight_copies(be_ref[jm]):
            cp.wait()
        w1b[...] = wf1[...].astype(BF16)
        w3b[...] = wf3[...].astype(BF16)
        w2b[...] = wf2[...].astype(BF16)

        @pl.when(nx_ref[jm] >= 0)
        def _():
            for cp in weight_copies(nx_ref[jm]):
                cp.start(priority=1)

    def gather_wait(p):
        for _ in range(MOE_TM):
            pltpu.make_async_copy(h_hbm.at[pl.ds(0, 1), :], xbufs[p].at[pl.ds(0, 1), :], gsem.at[p]).wait()

    def gather_start(p):
        for rr in range(MOE_TM):
            pltpu.make_async_copy(h_hbm.at[pl.ds(tok_ref[0, 0, rr], 1), :],
                                  xbufs[p].at[pl.ds(rr, 1), :], gsem.at[p]).start(priority=0)

    def compute(p):
        x = jnp.concatenate(_unpack_halves(xbufs[p][...]), axis=1).astype(BF16)
        a = _dot(x, w1b[...])
        b = _dot(x, w3b[...])
        hb = (a * jax.nn.sigmoid(a) * b).astype(BF16)
        y_ref[...] = _pack_halves(_dot(hb, w2b[...]))

    for p in range(N_GBUF):
        q = (p - LOOKAHEAD) % N_GBUF

        @pl.when(jnp.logical_and(jnp.logical_and(issue_ok, comp_ok), par == p))
        def _(p=p, q=q):
            gather_wait(q)
            gather_start(p)
            compute(q)

        @pl.when(jnp.logical_and(jnp.logical_and(jnp.logical_not(issue_ok), comp_ok), par == p))
        def _(q=q):
            gather_wait(q)
            compute(q)

    for p in range(LOOKAHEAD):
        @pl.when(jnp.logical_and(issue_ok, j == p))
        def _(p=p):
            gather_start(p)

    @pl.when(jnp.logical_and(j >= LOOKAHEAD, bv_ref[jm] == 0))
    def _():
        y_ref[...] = jnp.zeros_like(y_ref)


def _experts(h2p, slot_tok, blk_e, blk_valid, blk_next, w1, w3, w2, nb):
    ne, d, de = w1.shape
    dp = d // 2
    tok3 = slot_tok.reshape(nb, 1, MOE_TM)
    grid_spec = pltpu.PrefetchScalarGridSpec(
        num_scalar_prefetch=3,
        grid=(nb + LOOKAHEAD,),
        in_specs=[pl.BlockSpec((1, 1, MOE_TM), lambda j, be, bv, nx: (jnp.minimum(j, nb - 1), 0, 0),
                               memory_space=pltpu.SMEM),
                  pl.BlockSpec(memory_space=pl.ANY), pl.BlockSpec(memory_space=pl.ANY),
                  pl.BlockSpec(memory_space=pl.ANY), pl.BlockSpec(memory_space=pl.ANY)],
        out_specs=pl.BlockSpec((MOE_TM, dp), lambda j, be, bv, nx: (jnp.maximum(j - LOOKAHEAD, 0), 0)),
        scratch_shapes=[pltpu.VMEM((MOE_TM, dp), jnp.uint32)] * N_GBUF + [
                        pltpu.VMEM((d, de), F32), pltpu.VMEM((d, de), F32), pltpu.VMEM((de, d), F32),
                        pltpu.VMEM((d, de), BF16), pltpu.VMEM((d, de), BF16), pltpu.VMEM((de, d), BF16),
                        pltpu.SemaphoreType.DMA((N_GBUF,)), pltpu.SemaphoreType.DMA((3,))],
    )
    return pl.pallas_call(
        functools.partial(_expert_kernel, nb=nb),
        grid_spec=grid_spec,
        out_shape=jax.ShapeDtypeStruct((nb * MOE_TM, dp), jnp.uint32),
        compiler_params=_cparams(("arbitrary",)),
        name="experts",
    )(blk_e, blk_valid, blk_next, tok3, h2p, w1, w3, w2)


def _combine_kernel(pos_ref, y_hbm, h2_ref, xn_ref, wt_ref, mod_ref, gpost_ref,
                    w1_ref, w3_ref, w2_ref, o_ref, *scratch, n_tiles, k):
    j = pl.program_id(0)
    par = lax.rem(j, N_GBUF)
    n_rows = CMB_TM * k
    ybufs = scratch[:N_GBUF]
    sem = scratch[N_GBUF]

    def gather_wait(p):
        for _ in range(n_rows):
            pltpu.make_async_copy(y_hbm.at[pl.ds(0, 1), :], ybufs[p].at[pl.ds(0, 1), :], sem.at[p]).wait()

    def gather_start(p):
        for rr in range(n_rows):
            pltpu.make_async_copy(y_hbm.at[pl.ds(pos_ref[0, 0, rr], 1), :],
                                  ybufs[p].at[pl.ds(rr, 1), :], sem.at[p]).start(priority=rr % 2)

    def compute(p):
        x = jnp.concatenate(_unpack_halves(h2_ref[...]), axis=1).astype(BF16)
        a = _dot(x, w1_ref[...])
        b = _dot(x, w3_ref[...])
        f = _dot((a * jax.nn.sigmoid(a) * b).astype(BF16), w2_ref[...])
        wt = wt_ref[...]
        f_lo = jnp.zeros((CMB_TM, x.shape[1] // 2), F32)
        f_hi = f_lo
        for kk in range(k):
            lo, hi = _unpack_halves(ybufs[p][kk * CMB_TM:(kk + 1) * CMB_TM, :])
            f_lo = f_lo + wt[:, kk:kk + 1] * lo
            f_hi = f_hi + wt[:, kk:kk + 1] * hi
        f = f + jnp.concatenate([f_lo, f_hi], axis=1)
        mod = mod_ref[0]
        o_ref[...] = xn_ref[...] + mod[5:6] * _rms(f, gpost_ref[...])

    for p in range(LOOKAHEAD):
        @pl.when(j == p)
        def _(p=p):
            gather_start(p)

    for p in range(N_GBUF):
        q = (p - LOOKAHEAD) % N_GBUF

        @pl.when(jnp.logical_and(jnp.logical_and(j >= LOOKAHEAD, j < n_tiles), par == p))
        def _(p=p, q=q):
            gather_wait(q)
            gather_start(p)
            compute(q)

    for jj in range(n_tiles, n_tiles + LOOKAHEAD):
        @pl.when(j == jj)
        def _(jj=jj):
            gather_wait((jj - LOOKAHEAD) % N_GBUF)
            compute((jj - LOOKAHEAD) % N_GBUF)


def _combine(y, pos, wts, h2p, xn, mod, gpost, w1, w3, w2, n_ctx_rows, lat_len):
    t, d = xn.shape
    dp = d // 2
    k = pos.shape[1]
    n_tiles = t // CMB_TM
    pos3 = pos.reshape(n_tiles, CMB_TM, k).transpose(0, 2, 1).reshape(n_tiles, 1, CMB_TM * k)
    row = functools.partial(_seq_row, tile=CMB_TM, n_ctx_rows=n_ctx_rows, lat_len=lat_len)
    prev = lambda j: jnp.maximum(j - LOOKAHEAD, 0)
    full = lambda a: pl.BlockSpec(a.shape, lambda j, _n=a.ndim: (0,) * _n, pipeline_mode=pl.Buffered(1))
    rowspec = lambda w: pl.BlockSpec((CMB_TM, w), lambda j: (prev(j), 0))
    return pl.pallas_call(
        functools.partial(_combine_kernel, n_tiles=n_tiles, k=k),
        grid=(n_tiles + LOOKAHEAD,),
        in_specs=[pl.BlockSpec((1, 1, CMB_TM * k), lambda j: (jnp.minimum(j, n_tiles - 1), 0, 0),
                               memory_space=pltpu.SMEM),
                  pl.BlockSpec(memory_space=pl.ANY),
                  rowspec(dp), rowspec(d), rowspec(k),
                  pl.BlockSpec((1, 6, d), lambda j: (row(prev(j)), 0, 0)),
                  full(gpost), full(w1), full(w3), full(w2)],
        out_specs=rowspec(d),
        out_shape=jax.ShapeDtypeStruct((t, d), F32),
        scratch_shapes=[pltpu.VMEM((CMB_TM * k, dp), jnp.uint32)] * N_GBUF + [pltpu.SemaphoreType.DMA((N_GBUF,))],
        compiler_params=_cparams(("arbitrary",)),
        name="combine",
    )(pos3, y, h2p, xn, wts, mod, gpost, w1, w3, w2)


def _pad_cols(w, n):
    return jnp.pad(w, ((0, 0), (0, n - w.shape[1])))


def _pad_rows(w, n):
    return jnp.pad(w, ((0, n - w.shape[0]), (0, 0)))


def _pair_state(s):
    n, nd, h, hv, hk = s.shape
    s = s.reshape(n, nd, h // 2, 2, hv, hk)
    z = jnp.zeros_like(s[:, :, :, 0])
    top = jnp.concatenate([s[:, :, :, 0], z], axis=-1)
    bot = jnp.concatenate([z, s[:, :, :, 1]], axis=-1)
    return jnp.concatenate([top, bot], axis=-2)


def _unpair_state(s):
    n, nd, hp = s.shape[:3]
    a = s[..., :HEAD, :HEAD]
    b = s[..., HEAD:, HEAD:]
    return jnp.stack([a, b], axis=3).reshape(n, nd, hp * 2, HEAD, HEAD)


def kernel(x_prompt, x_sample, state_rwkv, c, c_ctx, w_ada, b_ada, g_pre_mix, g_post_mix, g_pre_ffn, g_post_ffn, w_in, conv_rkv, decay_a, decay_b, decay_bias, iclr_a, iclr_b, iclr_bias, gate_a, gate_b, k_k, k_a, r_k, ln_x_w, ln_x_b, w_fourier_out, w_rwkv_out, w_out, w_router, router_bias, w1_exp, w3_exp, w2_exp, w1_sh, w3_sh, w2_sh):
    n_ctx, l_ctx, d = x_prompt.shape
    n_lat, l_lat, _ = x_sample.shape
    depth = w_ada.shape[0]
    assert l_ctx == TM and l_lat % TM == 0 and l_lat % GRID_W == 0
    n_ctx_rows = n_ctx * l_ctx
    t = n_ctx_rows + n_lat * l_lat
    dr = k_k.shape[1]
    df = w_fourier_out.shape[1]
    n_fg = 4
    gw = df // n_fg
    n_heads = dr // HEAD
    ne = w_router.shape[2]
    lora_r = decay_a.shape[3]
    assert lora_r <= LANES and gate_a.shape[2] == 2 * LANES

    xc = x_prompt.reshape(n_ctx_rows, d)
    xl = x_sample.reshape(n_lat * l_lat, d)
    cond =jnp.concatenate([c_ctx[None, :], c, jnp.zeros((16 - 1 - n_lat, d), F32)], axis=0)
    states = []
    lat_state = state_rwkv
    for l in range(depth):
        mod = _ada(cond, w_ada[l].astype(BF16), b_ada[l][None, :]).reshape(16, 6, d)

        wi = w_in[l]
        lora_w = jnp.concatenate(
            [_pad_cols(decay_a[l, 0], LANES), _pad_cols(decay_a[l, 1], LANES),
             _pad_cols(iclr_a[l, 0], LANES), _pad_cols(iclr_a[l, 1], LANES), gate_a[l]], axis=1)
        lora_w = _pad_cols(lora_w, 1024)
        w_cat = jnp.concatenate([wi[:, df:df + 3 * dr], wi[:, :df], wi[:, df + 3 * dr:], lora_w], axis=1).astype(BF16)
        col_f = 3 * dr
        col_gf = col_f + df
        col_gr = col_gf + d
        col_lora = col_gr + d

        h = _prenorm(xc, xl, g_pre_mix[l][None, :], mod, l_lat)
        proj = _matmul(h, w_cat, 512, 1024, F32, "in_proj")

        yf_ctx = _fourier_ctx(proj, col_f, n_ctx, l_ctx, gw, n_fg)
        yf_lat = _fourier_lat(proj, col_f, n_ctx_rows, n_lat, l_lat, gw, n_fg)
        yf = jnp.concatenate([yf_ctx, yf_lat], axis=0)

        db = jnp.stack([_pad_rows(decay_b[l, z], LANES) for z in range(2)]).astype(BF16)
        ib = jnp.stack([_pad_rows(iclr_b[l, z], LANES) for z in range(2)]).astype(BF16)
        prep = _rwkv_prep(proj, col_lora, t, dr, n_ctx_rows, l_lat, conv_rkv[l], db, ib,
                          gate_b[l].astype(BF16), decay_bias[l], iclr_bias[l], k_k[l][None, :],
                          k_a[l][None, :], r_k[l].reshape(1, dr))
        scan_in, gate, bv = prep[:11], prep[11], prep[12]
        s0 = jnp.concatenate([jnp.zeros((n_ctx, 2, n_heads // 2, LANES, LANES), F32),
                              _pair_state(lat_state[:, l])], axis=0)
        per = l_lat // TM
        seq_tiles = [(b, 1) for b in range(n_ctx)] + [(n_ctx + b * per, per) for b in range(n_lat)]
        o_f, o_b, s_fin = _rwkv_scan(scan_in, s0, seq_tiles)
        states.append(_unpair_state(s_fin[:n_ctx]))
        yr = _rwkv_post(o_f, o_b, gate, bv, ln_x_w[l][None, :], ln_x_b[l][None, :])

        wr_hi = w_router[l].astype(BF16)
        wr_lo = (w_router[l] - wr_hi.astype(F32)).astype(BF16)
        xn, h2, scores = _mix(yf, yr, proj, col_gf, col_gr, xc, xl, mod, g_post_mix[l][None, :],
                              g_pre_ffn[l][None, :], w_fourier_out[l].astype(BF16),
                              w_rwkv_out[l].astype(BF16), w_out[l].astype(BF16), wr_hi, wr_lo, l_lat)

        eidx, wts = _route(scores, router_bias[l])
        slot_tok, blk_e, blk_valid, blk_next, pos, nb = _dispatch(eidx, ne)
        y = _experts(h2, slot_tok, blk_e, blk_valid, blk_next, w1_exp[l], w3_exp[l], w2_exp[l], nb)
        x = _combine(y, pos, wts, h2, xn, mod, g_post_ffn[l][None, :], w1_sh[l].astype(BF16),
                     w3_sh[l].astype(BF16), w2_sh[l].astype(BF16), n_ctx_rows, l_lat)
        xc, xl = x[:n_ctx_rows], x[n_ctx_rows:]

    return xc.reshape(n_ctx, l_ctx, d), xl.reshape(n_lat, l_lat, d), jnp.stack(states, axis=1)
```
